```python
import math
import jax, jax.numpy as jnp
from jax import lax
import numpy as np

D_MODEL = 1024
BATCH = 4
SEQ = 8192
DEPTH = 2

N_MEM = 256
GRID_W = 64
EPS = 1e-6
Q_BLOCK = 128
NEG_INF = -1e30

D_FF = 2816

MLA_HEADS = 8
MLA_Q_LORA = 256
MLA_KV_LORA = 128
MLA_NOPE = 64
MLA_ROPE = 32
MLA_QK = MLA_NOPE + MLA_ROPE
MLA_V = 64
ROPE_THETA = 10000.0

SWA_HEADS = 4
SWA_KV_HEADS = 2
SWA_GROUP = SWA_HEADS // SWA_KV_HEADS
SWA_DH = 64
SWA_WINDOW = 128

NA_HEADS = 4
NA_DH = 64
NA_KR = 8
NA_KC = 16

MEM_HEADS = 4
MEM_DH = 64

MLA_IN = MLA_Q_LORA + MLA_KV_LORA + MLA_ROPE
SWA_IN = (SWA_HEADS + 2 * SWA_KV_HEADS) * SWA_DH
NA_IN = 3 * NA_HEADS * NA_DH
MIX_IN = MLA_IN + SWA_IN + NA_IN
MLA_OUT = MLA_HEADS * MLA_V
SWA_OUT = SWA_HEADS * SWA_DH
NA_OUT = NA_HEADS * NA_DH
MIX_WIDTH = MLA_OUT + SWA_OUT + NA_OUT

kernel_name = "hybrid_mla_swa_natten_macaron_encoder"


def rms_norm(x, g):
    xf = x.astype(jnp.float32)
    y = xf * lax.rsqrt(jnp.mean(xf * xf, axis=-1, keepdims=True) + EPS)
    return (y * g.astype(jnp.float32)).astype(x.dtype)


def softmax_f32(s, dtype):
    return jax.nn.softmax(s.astype(jnp.float32), axis=-1).astype(dtype)


def swiglu(x, w_in, w_out):
    g, u = jnp.split(x @ w_in, 2, axis=-1)
    return (jax.nn.silu(g) * u) @ w_out


def alibi_slopes(n):
    return 2.0 ** (-8.0 * jnp.arange(1, n + 1, dtype=jnp.float32) / n)


def rope_tables(S):
    inv = 1.0 / (ROPE_THETA ** (jnp.arange(0, MLA_ROPE, 2, dtype=jnp.float32) / MLA_ROPE))
    ang = jnp.arange(S, dtype=jnp.float32)[:, None] * inv[None, :]
    return jnp.cos(ang), jnp.sin(ang)


def apply_rope(x, cos, sin):
    x1, x2 = jnp.split(x, 2, axis=-1)
    c = cos[None, :, None, :].astype(x.dtype)
    s = sin[None, :, None, :].astype(x.dtype)
    return jnp.concatenate([x1 * c - x2 * s, x1 * s + x2 * c], axis=-1)


def mla_mixer(z, q_norm_g, w_uq, kv_norm_g, w_ukv, q_gain, k_gain):
    B, S, _ = z.shape
    cq, ckv, k_rope = jnp.split(z, [MLA_Q_LORA, MLA_Q_LORA + MLA_KV_LORA], axis=-1)
    q = (rms_norm(cq, q_norm_g) @ w_uq).reshape(B, S, MLA_HEADS, MLA_QK)
    kv = (rms_norm(ckv, kv_norm_g) @ w_ukv).reshape(B, S, MLA_HEADS, MLA_NOPE + MLA_V)
    k_nope, v = jnp.split(kv, [MLA_NOPE], axis=-1)
    k_rope = jnp.broadcast_to(k_rope[:, :, None, :], (B, S, MLA_HEADS, MLA_ROPE))
    k = jnp.concatenate([k_nope, k_rope], axis=-1)
    q = rms_norm(q, q_gain)
    k = rms_norm(k, k_gain)
    cos, sin = rope_tables(S)
    q = jnp.concatenate([q[..., :MLA_NOPE], apply_rope(q[..., MLA_NOPE:], cos, sin)], axis=-1)
    k = jnp.concatenate([k[..., :MLA_NOPE], apply_rope(k[..., MLA_NOPE:], cos, sin)], axis=-1)
    scale = MLA_QK ** -0.5
    nb = S // Q_BLOCK
    qb = q.reshape(B, nb, Q_BLOCK, MLA_HEADS, MLA_QK).transpose(1, 0, 2, 3, 4)

    def block(qi):
        s = jnp.einsum('bqhd,bkhd->bhqk', qi, k).astype(jnp.float32) * scale
        p = softmax_f32(s, v.dtype)
        return jnp.einsum('bhqk,bkhv->bqhv', p, v)

    o = lax.map(block, qb)
    return o.transpose(1, 0, 2, 3, 4).reshape(B, S, MLA_OUT)


def swa_mixer(z, q_gain, k_gain, sink):
    B, S, _ = z.shape
    q, k, v = jnp.split(z, [SWA_HEADS * SWA_DH, (SWA_HEADS + SWA_KV_HEADS) * SWA_DH], axis=-1)
    q = rms_norm(q.reshape(B, S, SWA_HEADS, SWA_DH), q_gain)
    k = rms_norm(k.reshape(B, S, SWA_KV_HEADS, SWA_DH), k_gain)
    v = v.reshape(B, S, SWA_KV_HEADS, SWA_DH)
    nb = S // Q_BLOCK
    qb = q.reshape(B, nb, Q_BLOCK, SWA_KV_HEADS, SWA_GROUP, SWA_DH)

    def band(t):
        tp = jnp.pad(t, ((0, 0), (Q_BLOCK, Q_BLOCK), (0, 0), (0, 0)))
        tp = tp.reshape(B, nb + 2, Q_BLOCK, SWA_KV_HEADS, SWA_DH)
        return jnp.concatenate([tp[:, :nb], tp[:, 1:nb + 1], tp[:, 2:]], axis=2)

    kb, vb = band(k), band(v)
    s = jnp.einsum('bnqhgd,bnkhd->bnhgqk', qb, kb).astype(jnp.float32) * (SWA_DH ** -0.5)
    qpos = jnp.arange(S, dtype=jnp.int32).reshape(nb, Q_BLOCK)
    kpos = jnp.arange(nb, dtype=jnp.int32)[:, None] * Q_BLOCK - Q_BLOCK + jnp.arange(3 * Q_BLOCK, dtype=jnp.int32)[None, :]
    dist = jnp.abs(qpos[:, :, None] - kpos[:, None, :])
    valid = (dist <= SWA_WINDOW) & (kpos[:, None, :] >= 0) & (kpos[:, None, :] < S)
    slopes = alibi_slopes(SWA_HEADS).reshape(SWA_KV_HEADS, SWA_GROUP)
    s = s - slopes[None, None, :, :, None, None] * dist.astype(jnp.float32)[None, :, None, None, :, :]
    s = jnp.where(valid[None, :, None, None, :, :], s, NEG_INF)
    sink_col = jnp.broadcast_to(
        sink.astype(jnp.float32).reshape(SWA_KV_HEADS, SWA_GROUP)[None, None, :, :, None, None],
        s.shape[:-1] + (1,))
    p = softmax_f32(jnp.concatenate([s, sink_col], axis=-1), v.dtype)[..., :-1]
    o = jnp.einsum('bnhgqk,bnkhd->bnqhgd', p, vb)
    return o.reshape(B, S, SWA_OUT)


def na_mixer(z, q_gain, k_gain, rel_bias):
    B, S, _ = z.shape
    rows = S // GRID_W
    kr = min(NA_KR, rows)
    q, k, v = jnp.split(z, 3, axis=-1)
    q = rms_norm(q.reshape(B, rows, GRID_W, NA_HEADS, NA_DH), q_gain)
    k = rms_norm(k.reshape(B, rows, GRID_W, NA_HEADS, NA_DH), k_gain)
    v = v.reshape(B, rows, GRID_W, NA_HEADS, NA_DH)
    cols = jnp.arange(GRID_W, dtype=jnp.int32)
    c0 = jnp.clip(cols - NA_KC // 2, 0, GRID_W - NA_KC)
    col_idx = c0[:, None] + jnp.arange(NA_KC, dtype=jnp.int32)[None, :]
    dc_idx = col_idx - cols[:, None] + (NA_KC - 1)
    scale = NA_DH ** -0.5

    def row_block(args):
        r, q_r = args
        r0 = jnp.clip(r - kr // 2, 0, rows - kr)
        k_rows = lax.dynamic_slice_in_dim(k, r0, kr, axis=1)
        v_rows = lax.dynamic_slice_in_dim(v, r0, kr, axis=1)
        k_win = k_rows[:, :, col_idx]
        v_win = v_rows[:, :, col_idx]
        s = jnp.einsum('bqhd,brqkhd->bhqrk', q_r, k_win).astype(jnp.float32) * scale
        dr_idx = r0 + jnp.arange(kr, dtype=jnp.int32) - r + (NA_KR - 1)
        bias = rel_bias[:, dr_idx][:, :, dc_idx]
        s = s + bias.transpose(0, 2, 1, 3).astype(jnp.float32)[None]
        p = softmax_f32(s.reshape(B, NA_HEADS, GRID_W, kr * NA_KC), v.dtype)
        p = p.reshape(B, NA_HEADS, GRID_W, kr, NA_KC)
        return jnp.einsum('bhqrk,brqkhd->bqhd', p, v_win)

    o = lax.map(row_block, (jnp.arange(rows, dtype=jnp.int32), q.transpose(1, 0, 2, 3, 4)))
    return o.transpose(1, 0, 2, 3, 4).reshape(B, S, NA_OUT)


def mem_xattn(h, mem, mem_g, w_q, w_kv, q_gain, k_gain, w_o):
    B, S, _ = h.shape
    M = mem.shape[1]
    m = rms_norm(mem, mem_g)
    q = rms_norm((h @ w_q).reshape(B, S, MEM_HEADS, MEM_DH), q_gain)
    kv = (m @ w_kv).reshape(B, M, 2, MEM_HEADS, MEM_DH)
    k = rms_norm(kv[:, :, 0], k_gain)
    v = kv[:, :, 1]
    s = jnp.einsum('bshd,bmhd->bhsm', q, k).astype(jnp.float32) * (MEM_DH ** -0.5)
    p = softmax_f32(s, v.dtype)
    o = jnp.einsum('bhsm,bmhd->bshd', p, v).reshape(B, S, MEM_HEADS * MEM_DH)
    return o @ w_o


def setup_inputs(seed: int = 0) -> dict:
    key = jax.random.key(seed)
    ks = iter(jax.random.split(key, 40))
    L = DEPTH

    def w(shape, fan_in):
        return jax.random.normal(next(ks), shape, jnp.float32) * (fan_in ** -0.5)

    def gain(shape):
        return 1.0 + 0.02 * jax.random.normal(next(ks), shape, jnp.float32)

    return {
        "x": jax.random.normal(next(ks), (BATCH, SEQ, D_MODEL), jnp.float32),
        "mem": jax.random.normal(next(ks), (BATCH, N_MEM, D_MODEL), jnp.float32),
        "ffn1_norm": gain((L, D_MODEL)),
        "ffn1_w_in": w((L, D_MODEL, 2 * D_FF), D_MODEL),
        "ffn1_w_out": w((L, D_FF, D_MODEL), D_FF),
        "mix_norm": gain((L, D_MODEL)),
        "w_mix_in": w((L, D_MODEL, MIX_IN), D_MODEL),
        "mla_q_norm": gain((L, MLA_Q_LORA)),
        "mla_w_uq": w((L, MLA_Q_LORA, MLA_HEADS * MLA_QK), MLA_Q_LORA),
        "mla_kv_norm": gain((L, MLA_KV_LORA)),
        "mla_w_ukv": w((L, MLA_KV_LORA, MLA_HEADS * (MLA_NOPE + MLA_V)), MLA_KV_LORA),
        "mla_q_gain": gain((L, MLA_QK)),
        "mla_k_gain": gain((L, MLA_QK)),
        "swa_q_gain": gain((L, SWA_DH)),
        "swa_k_gain": gain((L, SWA_DH)),
        "swa_sink": 0.5 * jax.random.normal(next(ks), (L, SWA_HEADS), jnp.float32),
        "na_q_gain": gain((L, NA_DH)),
        "na_k_gain": gain((L, NA_DH)),
        "na_rel_bias": 0.1 * jax.random.normal(next(ks), (L, NA_HEADS, 2 * NA_KR - 1, 2 * NA_KC - 1), jnp.float32),
        "grp_out_gain": gain((L, MIX_WIDTH)),
        "w_mix_out": w((L, MIX_WIDTH, D_MODEL), MIX_WIDTH),
        "mem_norm_x": gain((L, D_MODEL)),
        "mem_norm_m": gain((L, D_MODEL)),
        "mem_w_q": w((L, D_MODEL, MEM_HEADS * MEM_DH), D_MODEL),
        "mem_w_kv": w((L, D_MODEL, 2 * MEM_HEADS * MEM_DH), D_MODEL),
        "mem_q_gain": gain((L, MEM_DH)),
        "mem_k_gain": gain((L, MEM_DH)),
        "mem_w_o": w((L, MEM_HEADS * MEM_DH, D_MODEL), MEM_HEADS * MEM_DH),
        "ffn2_norm": gain((L, D_MODEL)),
        "ffn2_w_in": w((L, D_MODEL, 2 * D_FF), D_MODEL),
        "ffn2_w_out": w((L, D_FF, D_MODEL), D_FF),
        "block_norm": gain((L, D_MODEL)),
    }


def reference(x, mem, ffn1_norm, ffn1_w_in, ffn1_w_out, mix_norm, w_mix_in,
              mla_q_norm, mla_w_uq, mla_kv_norm, mla_w_ukv, mla_q_gain, mla_k_gain,
              swa_q_gain, swa_k_gain, swa_sink, na_q_gain, na_k_gain, na_rel_bias,
              grp_out_gain, w_mix_out, mem_norm_x, mem_norm_m, mem_w_q, mem_w_kv,
              mem_q_gain, mem_k_gain, mem_w_o, ffn2_norm, ffn2_w_in, ffn2_w_out,
              block_norm):
    for l in range(DEPTH):
        x = x + 0.5 * swiglu(rms_norm(x, ffn1_norm[l]), ffn1_w_in[l], ffn1_w_out[l])
        h = rms_norm(x, mix_norm[l])
        z = h @ w_mix_in[l]
        z_mla, z_swa, z_na = jnp.split(z, [MLA_IN, MLA_IN + SWA_IN], axis=-1)
        o_mla = mla_mixer(z_mla, mla_q_norm[l], mla_w_uq[l], mla_kv_norm[l], mla_w_ukv[l],
                          mla_q_gain[l], mla_k_gain[l])
        o_swa = swa_mixer(z_swa, swa_q_gain[l], swa_k_gain[l], swa_sink[l])
        o_na = na_mixer(z_na, na_q_gain[l], na_k_gain[l], na_rel_bias[l])
        g = grp_out_gain[l]
        o = jnp.concatenate([
            rms_norm(o_mla, g[:MLA_OUT]),
            rms_norm(o_swa, g[MLA_OUT:MLA_OUT + SWA_OUT]),
            rms_norm(o_na, g[MLA_OUT + SWA_OUT:]),
        ], axis=-1)
        x = x + o @ w_mix_out[l]
        x = x + mem_xattn(rms_norm(x, mem_norm_x[l]), mem, mem_norm_m[l], mem_w_q[l], mem_w_kv[l],
                          mem_q_gain[l], mem_k_gain[l], mem_w_o[l])
        x = x + 0.5 * swiglu(rms_norm(x, ffn2_norm[l]), ffn2_w_in[l], ffn2_w_out[l])
        x = rms_norm(x, block_norm[l])
    return x
```

```python
import functools

import numpy as np
import jax
import jax.numpy as jnp
from jax import lax
from jax.experimental import pallas as pl
from jax.experimental.pallas import tpu as pltpu

F32 = jnp.float32
BF16 = jnp.bfloat16

D_MODEL = 1024
D_FF = 2816
N_MEM = 256
GRID_W = 64
EPS = 1e-6
NEG_INF = -1e30

MLA_HEADS = 8
MLA_Q_LORA = 256
MLA_KV_LORA = 128
MLA_NOPE = 64
MLA_ROPE = 32
MLA_QK = MLA_NOPE + MLA_ROPE
MLA_V = 64
ROPE_THETA = 10000.0
MLA_OUT = MLA_HEADS * MLA_V

SWA_HEADS = 4
SWA_KV_HEADS = 2
SWA_GROUP = SWA_HEADS // SWA_KV_HEADS
SWA_DH = 64
SWA_WINDOW = 128
SWA_OUT = SWA_HEADS * SWA_DH

NA_HEADS = 4
NA_DH = 64
NA_KR = 8
NA_KC = 16
NA_OUT = NA_HEADS * NA_DH

MEM_HEADS = 4
MEM_DH = 64

LANES = 128
V7X_VMEM_BYTES = 64 * 2**20
VMEM_LIMIT = 56 * 2**20

FFN_TM = 512
PREP_TM = 512
MLA_TQ = 256
MLA_TK = 512
SWA_TQ = 256
SWA_WIN = SWA_TQ + 2 * SWA_WINDOW
NA_TQ = 4 * GRID_W
NA_WROWS = 12
NA_WIN = NA_WROWS * GRID_W
OUT_TM = 512
MEM_TM = 512


def _cparams(sem):
    return pltpu.CompilerParams(dimension_semantics=sem, vmem_limit_bytes=VMEM_LIMIT)


def _rms(x, g):
    ms = jnp.mean(x * x, axis=-1, keepdims=True)
    return x * lax.rsqrt(ms + EPS) * g


def _dot(a, b):
    return jnp.dot(a, b, preferred_element_type=F32)


def _dot_nt(a, b):
    return lax.dot_general(a, b, (((1,), (1,)), ((), ())), preferred_element_type=F32)


def _resident(shape):
    nd = len(shape)
    return pl.BlockSpec(shape, lambda *_: (0,) * nd)


def _ffn_kernel(x_ref, g_ref, wg_ref, wu_ref, wo_ref, *rest, final_norm):
    if final_norm:
        fg_ref, o_ref = rest
    else:
        (o_ref,) = rest
    x = x_ref[...]
    xn = _rms(x, g_ref[...]).astype(BF16)
    g = _dot(xn, wg_ref[...])
    u = _dot(xn, wu_ref[...])
    a = (g * jax.nn.sigmoid(g) * u).astype(BF16)
    out = x + 0.5 * _dot(a, wo_ref[...])
    if final_norm:
        out = _rms(out, fg_ref[...])
    o_ref[...] = out


def _ffn(x2, g, wg, wu, wo, final_g=None):
    n = x2.shape[0]
    single = pl.Buffered(1)
    in_specs = [
        pl.BlockSpec((FFN_TM, D_MODEL), lambda i: (i, 0)),
        _resident((1, D_MODEL)),
        pl.BlockSpec((D_MODEL, D_FF), lambda i: (0, 0), pipeline_mode=single),
        pl.BlockSpec((D_MODEL, D_FF), lambda i: (0, 0), pipeline_mode=single),
        pl.BlockSpec((D_FF, D_MODEL), lambda i: (0, 0), pipeline_mode=single),
    ]
    args = [x2, g, wg, wu, wo]
    if final_g is not None:
        in_specs.append(_resident((1, D_MODEL)))
        args.append(final_g)
    return pl.pallas_call(
        functools.partial(_ffn_kernel, final_norm=final_g is not None),
        grid=(n // FFN_TM,),
        in_specs=in_specs,
        out_specs=pl.BlockSpec((FFN_TM, D_MODEL), lambda i: (i, 0)),
        out_shape=jax.ShapeDtypeStruct((n, D_MODEL), F32),
        compiler_params=_cparams(("parallel",)),
        name="ffn",
    )(*args)


def _prep_kernel(x_ref, mixg_ref, wmla_ref, qng_ref, kvng_ref, wuq_ref, wuk_ref, wuvt_ref,
                 qgain_ref, kgain_ref, rc_ref, rs1_ref, rs2_ref,
                 wswa_ref, sqg_ref, skg_ref, wna_ref, nqg_ref, nkg_ref, bd_ref,
                 q_ref, k_ref, vt_ref, sq_ref, sk_ref, sv_ref, nq_ref, nk_ref, nv_ref):
    xn = _rms(x_ref[0], mixg_ref[...]).astype(BF16)

    z = _dot(xn, wmla_ref[...])
    cqn = _rms(z[:, :MLA_Q_LORA], qng_ref[...]).astype(BF16)
    ckvn = _rms(z[:, MLA_Q_LORA:MLA_Q_LORA + MLA_KV_LORA], kvng_ref[...]).astype(BF16)
    kr = z[:, MLA_Q_LORA + MLA_KV_LORA:]
    qa = _dot(cqn, wuq_ref[...])
    ka = _dot(ckvn, wuk_ref[...])
    vt_ref[0, 0] = _dot_nt(wuvt_ref[...], ckvn).astype(BF16)

    rc, rs1, rs2 = rc_ref[...], rs1_ref[...], rs2_ref[...]
    qgain, kgain = qgain_ref[...], kgain_ref[...]

    def norm_rope(t, gain):
        ss = jnp.sum(t * t, axis=-1, keepdims=True)
        tn = t * lax.rsqrt(ss * (1.0 / MLA_QK) + EPS) * gain
        return tn * rc + pltpu.roll(tn, LANES - MLA_ROPE // 2, 1) * rs1 + pltpu.roll(tn, MLA_ROPE // 2, 1) * rs2

    for h in range(MLA_HEADS):
        sl = slice(h * LANES, (h + 1) * LANES)
        q_ref[0, h] = (norm_rope(qa[:, sl], qgain) * (MLA_QK ** -0.5)).astype(BF16)
        k_ref[0, h] = norm_rope(ka[:, sl] + kr, kgain).astype(BF16)

    bd = bd_ref[...]

    def head_norm(t, gain):
        ss = _dot((t * t).astype(BF16), bd)
        return t * lax.rsqrt(ss * (1.0 / SWA_DH) + EPS) * gain

    zs = _dot(xn, wswa_ref[...])
    sq_ref[0] = (head_norm(zs[:, :256], sqg_ref[...]) * (SWA_DH ** -0.5)).astype(BF16)
    sk_ref[0] = head_norm(zs[:, 256:512], skg_ref[...]).astype(BF16)
    sv_ref[0] = zs[:, 512:].astype(BF16)

    zn = _dot(xn, wna_ref[...])
    nq_ref[0] = (head_norm(zn[:, :256], nqg_ref[...]) * (NA_DH ** -0.5)).astype(BF16)
    nk_ref[0] = head_norm(zn[:, 256:512], nkg_ref[...]).astype(BF16)
    nv_ref[0] = zn[:, 512:].astype(BF16)


def _prep(x3, p):
    b, s, _ = x3.shape
    nt = s // PREP_TM
    tok = lambda w: pl.BlockSpec((1, PREP_TM, w), lambda bi, i: (bi, i, 0))
    rope = pl.BlockSpec((PREP_TM, LANES), lambda bi, i: (i, 0))
    weights = [p["mix_norm"], p["w_mla"], p["mla_q_norm"], p["mla_kv_norm"], p["w_uq"], p["w_uk"], p["w_uvt"],
               p["mla_q_gain"], p["mla_k_gain"]]
    tail = [p["w_swa"], p["swa_q_gain"], p["swa_k_gain"], p["w_na"], p["na_q_gain"], p["na_k_gain"], p["bd"]]
    in_specs = ([tok(D_MODEL)] + [_resident(w.shape) for w in weights] + [rope, rope, rope]
                + [_resident(w.shape) for w in tail])
    head_spec = pl.BlockSpec((1, MLA_HEADS, PREP_TM, LANES), lambda bi, i: (bi, 0, i, 0))
    out_specs = [head_spec, head_spec,
                 pl.BlockSpec((1, 1, MLA_OUT, PREP_TM), lambda bi, i: (bi, i, 0, 0))] + [tok(256)] * 6
    head_shape = jax.ShapeDtypeStruct((b, MLA_HEADS, s, LANES), BF16)
    tok_shape = jax.ShapeDtypeStruct((b, s, 256), BF16)
    out_shape = [head_shape, head_shape, jax.ShapeDtypeStruct((b, nt, MLA_OUT, PREP_TM), BF16)] + [tok_shape] * 6
    return pl.pallas_call(
        _prep_kernel,
        grid=(b, nt),
        in_specs=in_specs,
        out_specs=out_specs,
        out_shape=out_shape,
        compiler_params=_cparams(("parallel", "parallel")),
        name="mix_prep",
    )(x3, *weights, p["rope_c"], p["rope_s1"], p["rope_s2"], *tail)


def _mla_kernel(q_ref, k_ref, vt_ref, o_ref, *, nkb):
    tq = q_ref.shape[2]
    qs = [q_ref[0, 0], q_ref[0, 1]]

    def body(j, carry):
        row0 = pl.multiple_of(j * MLA_TK, MLA_TK)
        new = []
        for hh in range(2):
            m, l, acc = carry[hh]
            kb = k_ref[0, hh, pl.ds(row0, MLA_TK), :]
            s = _dot_nt(kb, qs[hh])
            m_new = jnp.maximum(m, jnp.max(s, axis=0, keepdims=True))
            alpha = jnp.exp(m - m_new)
            p = jnp.exp(s - m_new)
            l_new = alpha * l + jnp.sum(p, axis=0, keepdims=True)
            vt = vt_ref[0, j, hh * MLA_V:(hh + 1) * MLA_V, :]
            acc_new = alpha * acc + _dot(vt, p.astype(BF16))
            new.append((m_new, l_new, acc_new))
        return tuple(new)

    init = tuple((jnp.full((1, tq), NEG_INF, F32), jnp.zeros((1, tq), F32), jnp.zeros((MLA_V, tq), F32))
                 for _ in range(2))
    res = lax.fori_loop(0, nkb, body, init)
    o_t = jnp.concatenate([acc / l for (_, l, acc) in res], axis=0)
    o_ref[0] = o_t.T


def _mla_attn(q, k, vt):
    b, h, s, _ = q.shape
    nkb = s // MLA_TK
    return pl.pallas_call(
        functools.partial(_mla_kernel, nkb=nkb),
        grid=(b, h // 2, s // MLA_TQ),
        in_specs=[
            pl.BlockSpec((1, 2, MLA_TQ, LANES), lambda bi, hp, i: (bi, hp, i, 0)),
            pl.BlockSpec((1, 2, s, LANES), lambda bi, hp, i: (bi, hp, 0, 0)),
            pl.BlockSpec((1, nkb, 2 * MLA_V, MLA_TK), lambda bi, hp, i: (bi, 0, hp, 0)),
        ],
        out_specs=pl.BlockSpec((1, MLA_TQ, 2 * MLA_V), lambda bi, hp, i: (bi, i, hp)),
        out_shape=jax.ShapeDtypeStruct((b, s, MLA_OUT), F32),
        compiler_params=_cparams(("parallel", "parallel", "arbitrary")),
        name="mla_attn",
    )(q, k, vt.reshape(b, nkb, MLA_OUT, MLA_TK))


def _lane_head_mask(h):
    lane = lax.broadcasted_iota(jnp.int32, (1, 256), 1)
    return (lane >= h * 64) & (lane < (h + 1) * 64)


def _window_attn(q, kw, vw, bias_ref, sinks):
    o = jnp.zeros((q.shape[0], 256), F32)
    for h in range(4):
        mask = _lane_head_mask(h)
        s = _dot_nt(jnp.where(mask, q, jnp.zeros_like(q)), kw) + bias_ref[0, h]
        m = jnp.max(s, axis=-1, keepdims=True)
        if sinks is not None:
            m = jnp.maximum(m, sinks[h])
        p = jnp.exp(s - m)
        l = jnp.sum(p, axis=-1, keepdims=True)
        if sinks is not None:
            l = l + jnp.exp(sinks[h] - m)
        o = o + _dot((p / l).astype(BF16), jnp.where(mask, vw, jnp.zeros_like(vw)))
    return o


def _swa_kernel(sink_ref, q_ref, k_ref, v_ref, bias_ref, g_ref, o_ref, *, seq):
    i = pl.program_id(1)
    w0 = pl.multiple_of(jnp.clip(i * SWA_TQ - SWA_WINDOW, 0, seq - SWA_WIN), SWA_WINDOW)
    kw = k_ref[0, pl.ds(w0, SWA_WIN), :]
    vw = v_ref[0, pl.ds(w0, SWA_WIN), :]
    sinks = [sink_ref[h] for h in range(SWA_HEADS)]
    o = _window_attn(q_ref[0], kw, vw, bias_ref, sinks)
    o_ref[0] = _rms(o, g_ref[...]).astype(BF16)


def _na_kernel(q_ref, k_ref, v_ref, bias_ref, g_ref, o_ref, *, rows):
    i = pl.program_id(1)
    r0 = jnp.clip(i * (NA_TQ // GRID_W) - NA_KR // 2, 0, rows - NA_WROWS)
    w0 = pl.multiple_of(r0 * GRID_W, GRID_W)
    kw = k_ref[0, pl.ds(w0, NA_WIN), :]
    vw = v_ref[0, pl.ds(w0, NA_WIN), :]
    o = _window_attn(q_ref[0], kw, vw, bias_ref, None)
    o_ref[0] = _rms(o, g_ref[...]).astype(BF16)


def _edge_pattern(i, n):
    return (i > 0).astype(jnp.int32) + (i == n - 1).astype(jnp.int32)


def _swa_attn(q, k, v, bias, sink, g):
    b, s, _ = q.shape
    nq = s // SWA_TQ
    return pl.pallas_call(
        functools.partial(_swa_kernel, seq=s),
        grid=(b, nq),
        in_specs=[
            pl.BlockSpec(memory_space=pltpu.SMEM),
            pl.BlockSpec((1, SWA_TQ, 256), lambda bi, i: (bi, i, 0)),
            pl.BlockSpec((1, s, 256), lambda bi, i: (bi, 0, 0)),
            pl.BlockSpec((1, s, 256), lambda bi, i: (bi, 0, 0)),
            pl.BlockSpec((1, SWA_HEADS, SWA_TQ, SWA_WIN), lambda bi, i: (_edge_pattern(i, nq), 0, 0, 0)),
            _resident((1, SWA_OUT)),
        ],
        out_specs=pl.BlockSpec((1, SWA_TQ, 256), lambda bi, i: (bi, i, 0)),
        out_shape=jax.ShapeDtypeStruct((b, s, SWA_OUT), BF16),
        compiler_params=_cparams(("parallel", "arbitrary")),
        name="swa_attn",
    )(sink, q, k, v, bias, g)


def _na_attn(q, k, v, bias, g):
    b, s, _ = q.shape
    nq = s // NA_TQ
    return pl.pallas_call(
        functools.partial(_na_kernel, rows=s // GRID_W),
        grid=(b, nq),
        in_specs=[
            pl.BlockSpec((1, NA_TQ, 256), lambda bi, i: (bi, i, 0)),
            pl.BlockSpec((1, s, 256), lambda bi, i: (bi, 0, 0)),
            pl.BlockSpec((1, s, 256), lambda bi, i: (bi, 0, 0)),
            pl.BlockSpec((1, NA_HEADS, NA_TQ, NA_WIN), lambda bi, i: (_edge_pattern(i, nq), 0, 0, 0)),
            _resident((1, NA_OUT)),
        ],
        out_specs=pl.BlockSpec((1, NA_TQ, 256), lambda bi, i: (bi, i, 0)),
        out_shape=jax.ShapeDtypeStruct((b, s, NA_OUT), BF16),
        compiler_params=_cparams(("parallel", "arbitrary")),
        name="na_attn",
    )(q, k, v, bias, g)


def _swa_bias_table(seq):
    slopes = 2.0 ** (-8.0 * np.arange(1, SWA_HEADS + 1, dtype=np.float64) / SWA_HEADS)
    ql = np.arange(SWA_TQ)[:, None]
    kl = np.arange(SWA_WIN)[None, :]
    pats = []
    for q0, w0 in ((0, 0), (SWA_TQ, SWA_TQ - SWA_WINDOW), (seq - SWA_TQ, seq - SWA_WIN)):
        dist = np.abs(ql + q0 - (kl + w0))
        pats.append(np.where(dist[None] <= SWA_WINDOW, -slopes[:, None, None] * dist[None], NEG_INF))
    return jnp.asarray(np.stack(pats), F32)


def _na_bias_table(rel_bias, rows):
    qr_l = np.arange(NA_TQ // GRID_W)
    kr_l = np.arange(NA_WROWS)
    qc = np.arange(GRID_W)[:, None]
    kc = np.arange(GRID_W)[None, :]
    c0 = np.clip(qc - NA_KC // 2, 0, GRID_W - NA_KC)
    col_ok = (kc >= c0) & (kc < c0 + NA_KC)
    dc = np.clip(kc - qc + (NA_KC - 1), 0, 2 * NA_KC - 2)
    last_first = rows - NA_TQ // GRID_W
    pats = []
    for r_first, w_row0 in ((0, 0), (NA_KR // 2, 0), (last_first, rows - NA_WROWS)):
        rq = r_first + qr_l[:, None]
        rk = w_row0 + kr_l[None, :]
        r0 = np.clip(rq - NA_KR // 2, 0, rows - NA_KR)
        row_ok = (rk >= r0) & (rk < r0 + NA_KR)
        dr = np.clip(rk - rq + (NA_KR - 1), 0, 2 * NA_KR - 2)
        t = rel_bias[:, dr]
        t = t[:, :, :, dc]
        ok = row_ok[:, :, None, None] & col_ok[None, None]
        t = jnp.where(ok[None], t, NEG_INF)
        pats.append(t.transpose(0, 1, 3, 2, 4).reshape(NA_HEADS, NA_TQ, NA_WIN))
    return jnp.stack(pats).astype(F32)


def _outproj_kernel(x_ref, om_ref, os_ref, on_ref, g_ref, w_ref, o_ref):
    om = _rms(om_ref[...], g_ref[...]).astype(BF16)
    cat = jnp.concatenate([om, os_ref[...], on_ref[...]], axis=-1)
    o_ref[...] = x_ref[...] + _dot(cat, w_ref[...])


def _out_proj(x2, o_mla, o_swa, o_na, g_mla, w):
    n = x2.shape[0]
    row = lambda w_: pl.BlockSpec((OUT_TM, w_), lambda i: (i, 0))
    return pl.pallas_call(
        _outproj_kernel,
        grid=(n // OUT_TM,),
        in_specs=[row(D_MODEL), row(MLA_OUT), row(SWA_OUT), row(NA_OUT), _resident((1, MLA_OUT)),
                  _resident(w.shape)],
        out_specs=row(D_MODEL),
        out_shape=jax.ShapeDtypeStruct((n, D_MODEL), F32),
        compiler_params=_cparams(("parallel",)),
        name="out_proj",
    )(x2, o_mla, o_swa, o_na, g_mla, w)


def _memkv_kernel(mem_ref, g_ref, w_ref, kg_ref, bd_ref, k_ref, v_ref):
    m = _rms(mem_ref[0], g_ref[...]).astype(BF16)
    kv = _dot(m, w_ref[...])
    k = kv[:, :256]
    ss = _dot((k * k).astype(BF16), bd_ref[...])
    k = (k * lax.rsqrt(ss * (1.0 / MEM_DH) + EPS) * kg_ref[...]).astype(BF16)
    v = kv[:, 256:].astype(BF16)
    for h in range(MEM_HEADS):
        mask = _lane_head_mask(h)
        k_ref[0, h] = jnp.where(mask, k, jnp.zeros_like(k))
        v_ref[0, h] = jnp.where(mask, v, jnp.zeros_like(v))


def _mem_kv(mem, g, w, kg, bd):
    b, m, _ = mem.shape
    out = jax.ShapeDtypeStruct((b, MEM_HEADS, m, 256), BF16)
    spec = pl.BlockSpec((1, MEM_HEADS, m, 256), lambda bi: (bi, 0, 0, 0))
    return pl.pallas_call(
        _memkv_kernel,
        grid=(b,),
        in_specs=[pl.BlockSpec((1, m, D_MODEL), lambda bi: (bi, 0, 0)), _resident(g.shape), _resident(w.shape),
                  _resident(kg.shape), _resident(bd.shape)],
        out_specs=[spec, spec],
        out_shape=[out, out],
        compiler_params=_cparams(("parallel",)),
        name="mem_kv",
    )(mem, g, w, kg, bd)


def _memattn_kernel(x_ref, g_ref, wq_ref, qg_ref, bd_ref, k_ref, v_ref, wo_ref, o_ref):
    x = x_ref[0]
    xn = _rms(x, g_ref[...]).astype(BF16)
    q = _dot(xn, wq_ref[...])
    ss = _dot((q * q).astype(BF16), bd_ref[...])
    q = (q * lax.rsqrt(ss * (1.0 / MEM_DH) + EPS) * qg_ref[...] * (MEM_DH ** -0.5)).astype(BF16)
    o = jnp.zeros((x.shape[0], 256), F32)
    for h in range(MEM_HEADS):
        s = _dot_nt(q, k_ref[0, h])
        m = jnp.max(s, axis=-1, keepdims=True)
        p = jnp.exp(s - m)
        l = jnp.sum(p, axis=-1, keepdims=True)
        o = o + _dot((p / l).astype(BF16), v_ref[0, h])
    o_ref[0] = x + _dot(o.astype(BF16), wo_ref[...])


def _mem_attn(x3, g, wq, qg, bd, k, v, wo):
    b, s, _ = x3.shape
    m = k.shape[2]
    tok = pl.BlockSpec((1, MEM_TM, D_MODEL), lambda bi, i: (bi, i, 0))
    kv = pl.BlockSpec((1, MEM_HEADS, m, 256), lambda bi, i: (bi, 0, 0, 0))
    return pl.pallas_call(
        _memattn_kernel,
        grid=(b, s // MEM_TM),
        in_specs=[tok, _resident(g.shape), _resident(wq.shape), _resident(qg.shape), _resident(bd.shape), kv, kv,
                  _resident(wo.shape)],
        out_specs=tok,
        out_shape=jax.ShapeDtypeStruct((b, s, D_MODEL), F32),
        compiler_params=_cparams(("parallel", "parallel")),
        name="mem_attn",
    )(x3, g, wq, qg, bd, k, v, wo)


def _rope_tables(seq):
    half = MLA_ROPE // 2
    inv = 1.0 / (ROPE_THETA ** (jnp.arange(0, MLA_ROPE, 2, dtype=F32) / MLA_ROPE))
    ang = jnp.arange(seq, dtype=F32)[:, None] * inv[None, :]
    cos, sin = jnp.cos(ang), jnp.sin(ang)
    zeros = lambda w: jnp.zeros((seq, w), F32)
    pad = LANES - MLA_QK
    rc = jnp.concatenate([jnp.ones((seq, MLA_NOPE), F32), cos, cos, zeros(pad)], axis=1)
    rs1 = jnp.concatenate([zeros(MLA_NOPE), -sin, zeros(half), zeros(pad)], axis=1)
    rs2 = jnp.concatenate([zeros(MLA_NOPE), zeros(half), sin, zeros(pad)], axis=1)
    return rc, rs1, rs2


def _pad_lanes(a, width=LANES):
    return jnp.pad(a, [(0, 0)] * (a.ndim - 1) + [(0, width - a.shape[-1])])


def _layer_params(l, seq, a):
    row = lambda v: v[l].reshape(1, -1).astype(F32)
    p = {}
    for name in ("ffn1", "ffn2"):
        w_in = a[name + "_w_in"][l]
        p[name] = (row(a[name + "_norm"]), w_in[:, :D_FF].astype(BF16), w_in[:, D_FF:].astype(BF16),
                   a[name + "_w_out"][l].astype(BF16))
    w = a["w_mix_in"][l]
    o_ckv = MLA_Q_LORA
    o_kr = o_ckv + MLA_KV_LORA
    o_swa = o_kr + MLA_ROPE
    o_na = o_swa + (SWA_HEADS + 2 * SWA_KV_HEADS) * SWA_DH
    w_kr = jnp.pad(w[:, o_kr:o_swa], ((0, 0), (MLA_NOPE, LANES - MLA_QK)))
    p["w_mla"] = jnp.concatenate([w[:, :o_kr], w_kr], axis=1).astype(BF16)
    p["mix_norm"] = row(a["mix_norm"])
    p["mla_q_norm"] = row(a["mla_q_norm"])
    p["mla_kv_norm"] = row(a["mla_kv_norm"])
    w_uq = a["mla_w_uq"][l].reshape(MLA_Q_LORA, MLA_HEADS, MLA_QK)
    p["w_uq"] = _pad_lanes(w_uq).reshape(MLA_Q_LORA, MLA_HEADS * LANES).astype(BF16)
    w_ukv = a["mla_w_ukv"][l].reshape(MLA_KV_LORA, MLA_HEADS, MLA_NOPE + MLA_V)
    p["w_uk"] = _pad_lanes(w_ukv[:, :, :MLA_NOPE]).reshape(MLA_KV_LORA, MLA_HEADS * LANES).astype(BF16)
    p["w_uvt"] = w_ukv[:, :, MLA_NOPE:].reshape(MLA_KV_LORA, MLA_OUT).T.astype(BF16)
    p["mla_q_gain"] = _pad_lanes(row(a["mla_q_gain"]))
    p["mla_k_gain"] = _pad_lanes(row(a["mla_k_gain"]))
    p["rope_c"], p["rope_s1"], p["rope_s2"] = _rope_tables(seq)
    w_sq = w[:, o_swa:o_swa + SWA_OUT]
    w_sk = w[:, o_swa + SWA_OUT:o_swa + SWA_OUT + SWA_KV_HEADS * SWA_DH].reshape(D_MODEL, SWA_KV_HEADS, SWA_DH)
    w_sv = w[:, o_swa + SWA_OUT + SWA_KV_HEADS * SWA_DH:o_na].reshape(D_MODEL, SWA_KV_HEADS, SWA_DH)
    expand = lambda t: jnp.repeat(t, SWA_GROUP, axis=1).reshape(D_MODEL, SWA_OUT)
    p["w_swa"] = jnp.concatenate([w_sq, expand(w_sk), expand(w_sv)], axis=1).astype(BF16)
    p["swa_q_gain"] = jnp.tile(row(a["swa_q_gain"]), (1, SWA_HEADS))
    p["swa_k_gain"] = jnp.tile(row(a["swa_k_gain"]), (1, SWA_HEADS))
    p["swa_sink"] = a["swa_sink"][l].astype(F32)
    p["w_na"] = w[:, o_na:].astype(BF16)
    p["na_q_gain"] = jnp.tile(row(a["na_q_gain"]), (1, NA_HEADS))
    p["na_k_gain"] = jnp.tile(row(a["na_k_gain"]), (1, NA_HEADS))
    p["na_bias"] = _na_bias_table(a["na_rel_bias"][l].astype(F32), seq // GRID_W)
    g = row(a["grp_out_gain"])
    p["g_mla"], p["g_swa"], p["g_na"] = g[:, :MLA_OUT], g[:, MLA_OUT:MLA_OUT + SWA_OUT], g[:, MLA_OUT + SWA_OUT:]
    p["w_mix_out"] = a["w_mix_out"][l].astype(BF16)
    p["mem_norm_x"] = row(a["mem_norm_x"])
    p["mem_norm_m"] = row(a["mem_norm_m"])
    p["mem_w_q"] = a["mem_w_q"][l].astype(BF16)
    p["mem_w_kv"] = a["mem_w_kv"][l].astype(BF16)
    p["mem_q_gain"] = jnp.tile(row(a["mem_q_gain"]), (1, MEM_HEADS))
    p["mem_k_gain"] = jnp.tile(row(a["mem_k_gain"]), (1, MEM_HEADS))
    p["mem_w_o"] = a["mem_w_o"][l].astype(BF16)
    p["block_norm"] = row(a["block_norm"])
    seg = np.arange(256) // 64
    p["bd"] = jnp.asarray(seg[:, None] == seg[None, :], BF16)
    return p


def kernel(x, mem, ffn1_norm, ffn1_w_in, ffn1_w_out, mix_norm, w_mix_in, mla_q_norm, mla_w_uq, mla_kv_norm, mla_w_ukv, mla_q_gain, mla_k_gain, swa_q_gain, swa_k_gain, swa_sink, na_q_gain, na_k_gain, na_rel_bias, grp_out_gain, w_mix_out, mem_norm_x, mem_norm_m, mem_w_q, mem_w_kv, mem_q_gain, mem_k_gain, mem_w_o, ffn2_norm, ffn2_w_in, ffn2_w_out, block_norm):
    a = dict(ffn1_norm=ffn1_norm, ffn1_w_in=ffn1_w_in, ffn1_w_out=ffn1_w_out, mix_norm=mix_norm, w_mix_in=w_mix_in,
             mla_q_norm=mla_q_norm, mla_w_uq=mla_w_uq, mla_kv_norm=mla_kv_norm, mla_w_ukv=mla_w_ukv,
             mla_q_gain=mla_q_gain, mla_k_gain=mla_k_gain, swa_q_gain=swa_q_gain, swa_k_gain=swa_k_gain,
             swa_sink=swa_sink, na_q_gain=na_q_gain, na_k_gain=na_k_gain, na_rel_bias=na_rel_bias,
             grp_out_gain=grp_out_gain, w_mix_out=w_mix_out, mem_norm_x=mem_norm_x, mem_norm_m=mem_norm_m,
             mem_w_q=mem_w_q, mem_w_kv=mem_w_kv, mem_q_gain=mem_q_gain, mem_k_gain=mem_k_gain, mem_w_o=mem_w_o,
             ffn2_norm=ffn2_norm, ffn2_w_in=ffn2_w_in, ffn2_w_out=ffn2_w_out, block_norm=block_norm)
    b, s, d = x.shape
    depth = ffn1_norm.shape[0]
    swa_bias = _swa_bias_table(s)
    x = x.astype(F32)
    mem = mem.astype(F32)
    for l in range(depth):
        p = _layer_params(l, s, a)
        x2 = _ffn(x.reshape(b * s, d), *p["ffn1"])
        q, k, vt, sq, sk, sv, nq, nk, nv = _prep(x2.reshape(b, s, d), p)
        o_mla = _mla_attn(q, k, vt)
        o_swa = _swa_attn(sq, sk, sv, swa_bias, p["swa_sink"], p["g_swa"])
        o_na = _na_attn(nq, nk, nv, p["na_bias"], p["g_na"])
        x2 = _out_proj(x2, o_mla.reshape(b * s, MLA_OUT), o_swa.reshape(b * s, SWA_OUT),
                       o_na.reshape(b * s, NA_OUT), p["g_mla"], p["w_mix_out"])
        mk, mv = _mem_kv(mem, p["mem_norm_m"], p["mem_w_kv"], p["mem_k_gain"], p["bd"])
        x3 = _mem_attn(x2.reshape(b, s, d), p["mem_norm_x"], p["mem_w_q"], p["mem_q_gain"], p["bd"], mk, mv,
                       p["mem_w_o"])
        x = _ffn(x3.reshape(b * s, d), *p["ffn2"], final_g=p["block_norm"])
    return x.reshape(b, s, d)
```

```python
import functools

import numpy as np
import jax
import jax.numpy as jnp
from jax import lax
from jax.experimental import pallas as pl
from jax.experimental.pallas import tpu as pltpu

F32 = jnp.float32
BF16 = jnp.bfloat16

D_MODEL = 1024
D_FF = 2816
N_MEM = 256
GRID_W = 64
EPS = 1e-6
NEG_INF = -1e30
LOG2_E = 1.4426950408889634

MLA_HEADS = 8
MLA_Q_LORA = 256
MLA_KV_LORA = 128
MLA_NOPE = 64
MLA_ROPE = 32
MLA_QK = MLA_NOPE + MLA_ROPE
MLA_V = 64
ROPE_THETA = 10000.0
MLA_OUT = MLA_HEADS * MLA_V

SWA_HEADS = 4
SWA_KV_HEADS = 2
SWA_GROUP = SWA_HEADS // SWA_KV_HEADS
SWA_DH = 64
SWA_WINDOW = 128
SWA_OUT = SWA_HEADS * SWA_DH

NA_HEADS = 4
NA_DH = 64
NA_KR = 8
NA_KC = 16
NA_OUT = NA_HEADS * NA_DH

MEM_HEADS = 4
MEM_DH = 64

LANES = 128
V7X_VMEM_BYTES = 64 * 2**20
VMEM_LIMIT = 56 * 2**20

FFN_TM = 512
PREP_TM = 512
MLA_TQ = 256
MLA_TK = 512
SWA_TQ = 256
SWA_WIN = SWA_TQ + 2 * SWA_WINDOW
NA_TQ = 4 * GRID_W
NA_WROWS = 12
NA_WIN = NA_WROWS * GRID_W
OUT_TM = 512
MEM_TM = 512


def _cparams(sem):
    return pltpu.CompilerParams(dimension_semantics=sem, vmem_limit_bytes=VMEM_LIMIT)


def _rms(x, g):
    ms = jnp.mean(x * x, axis=-1, keepdims=True)
    return x * lax.rsqrt(ms + EPS) * g


def _dot(a, b):
    return jnp.dot(a, b, preferred_element_type=F32)


def _dot_nt(a, b):
    return lax.dot_general(a, b, (((1,), (1,)), ((), ())), preferred_element_type=F32)


def _resident(shape):
    nd = len(shape)
    return pl.BlockSpec(shape, lambda *_: (0,) * nd)


def _ffn_kernel(x_ref, g_ref, wg_ref, wu_ref, wo_ref, *rest, final_norm):
    if final_norm:
        fg_ref, o_ref = rest
    else:
        (o_ref,) = rest
    x = x_ref[...]
    xn = _rms(x, g_ref[...]).astype(BF16)
    g = _dot(xn, wg_ref[...])
    u = _dot(xn, wu_ref[...])
    a = (g * jax.nn.sigmoid(g) * u).astype(BF16)
    out = x + 0.5 * _dot(a, wo_ref[...])
    if final_norm:
        out = _rms(out, fg_ref[...])
    o_ref[...] = out


def _ffn(x2, g, wg, wu, wo, final_g=None):
    n = x2.shape[0]
    single = pl.Buffered(1)
    in_specs = [
        pl.BlockSpec((FFN_TM, D_MODEL), lambda i: (i, 0)),
        _resident((1, D_MODEL)),
        pl.BlockSpec((D_MODEL, D_FF), lambda i: (0, 0), pipeline_mode=single),
        pl.BlockSpec((D_MODEL, D_FF), lambda i: (0, 0), pipeline_mode=single),
        pl.BlockSpec((D_FF, D_MODEL), lambda i: (0, 0), pipeline_mode=single),
    ]
    args = [x2, g, wg, wu, wo]
    if final_g is not None:
        in_specs.append(_resident((1, D_MODEL)))
        args.append(final_g)
    return pl.pallas_call(
        functools.partial(_ffn_kernel, final_norm=final_g is not None),
        grid=(n // FFN_TM,),
        in_specs=in_specs,
        out_specs=pl.BlockSpec((FFN_TM, D_MODEL), lambda i: (i, 0)),
        out_shape=jax.ShapeDtypeStruct((n, D_MODEL), F32),
        compiler_params=_cparams(("parallel",)),
        name="ffn",
    )(*args)


def _prep_kernel(x_ref, mixg_ref, wmla_ref, qng_ref, kvng_ref, wuq_ref, wuk_ref, wuvt_ref,
                 qgain_ref, kgain_ref, rc_ref, rs1_ref, rs2_ref,
                 wswa_ref, sqg_ref, skg_ref, wna_ref, nqg_ref, nkg_ref, bd_ref,
                 q_ref, k_ref, vt_ref, sq_ref, sk_ref, sv_ref, nq_ref, nk_ref, nv_ref):
    xn = _rms(x_ref[0], mixg_ref[...]).astype(BF16)

    z = _dot(xn, wmla_ref[...])
    cqn = _rms(z[:, :MLA_Q_LORA], qng_ref[...]).astype(BF16)
    ckvn = _rms(z[:, MLA_Q_LORA:MLA_Q_LORA + MLA_KV_LORA], kvng_ref[...]).astype(BF16)
    kr = z[:, MLA_Q_LORA + MLA_KV_LORA:]
    qa = _dot(cqn, wuq_ref[...])
    ka = _dot(ckvn, wuk_ref[...])
    vt_ref[0, 0] = _dot_nt(wuvt_ref[...], ckvn).astype(BF16)

    rc, rs1, rs2 = rc_ref[...], rs1_ref[...], rs2_ref[...]
    qgain, kgain = qgain_ref[...], kgain_ref[...]

    def norm_rope(t, gain):
        ss = jnp.sum(t * t, axis=-1, keepdims=True)
        tn = t * lax.rsqrt(ss * (1.0 / MLA_QK) + EPS) * gain
        return tn * rc + pltpu.roll(tn, LANES - MLA_ROPE // 2, 1) * rs1 + pltpu.roll(tn, MLA_ROPE // 2, 1) * rs2

    for h in range(MLA_HEADS):
        sl = slice(h * LANES, (h + 1) * LANES)
        q_ref[0, h] = (norm_rope(qa[:, sl], qgain) * (MLA_QK ** -0.5 * LOG2_E)).astype(BF16)
        k_ref[0, h] = norm_rope(ka[:, sl] + kr, kgain).astype(BF16)

    bd = bd_ref[...]

    def head_norm(t, gain):
        ss = _dot((t * t).astype(BF16), bd)
        return t * lax.rsqrt(ss * (1.0 / SWA_DH) + EPS) * gain

    zs = _dot(xn, wswa_ref[...])
    sq_ref[0] = (head_norm(zs[:, :256], sqg_ref[...]) * (SWA_DH ** -0.5)).astype(BF16)
    sk_ref[0] = head_norm(zs[:, 256:512], skg_ref[...]).astype(BF16)
    sv_ref[0] = zs[:, 512:].astype(BF16)

    zn = _dot(xn, wna_ref[...])
    nq_ref[0] = (head_norm(zn[:, :256], nqg_ref[...]) * (NA_DH ** -0.5)).astype(BF16)
    nk_ref[0] = head_norm(zn[:, 256:512], nkg_ref[...]).astype(BF16)
    nv_ref[0] = zn[:, 512:].astype(BF16)


def _prep(x3, p):
    b, s, _ = x3.shape
    nt = s // PREP_TM
    tok = lambda w: pl.BlockSpec((1, PREP_TM, w), lambda bi, i: (bi, i, 0))
    rope = pl.BlockSpec((PREP_TM, LANES), lambda bi, i: (i, 0))
    weights = [p["mix_norm"], p["w_mla"], p["mla_q_norm"], p["mla_kv_norm"], p["w_uq"], p["w_uk"], p["w_uvt"],
               p["mla_q_gain"], p["mla_k_gain"]]
    tail = [p["w_swa"], p["swa_q_gain"], p["swa_k_gain"], p["w_na"], p["na_q_gain"], p["na_k_gain"], p["bd"]]
    in_specs = ([tok(D_MODEL)] + [_resident(w.shape) for w in weights] + [rope, rope, rope]
                + [_resident(w.shape) for w in tail])
    head_spec = pl.BlockSpec((1, MLA_HEADS, PREP_TM, LANES), lambda bi, i: (bi, 0, i, 0))
    out_specs = [head_spec, head_spec,
                 pl.BlockSpec((1, 1, MLA_OUT, PREP_TM), lambda bi, i: (bi, i, 0, 0))] + [tok(256)] * 6
    head_shape = jax.ShapeDtypeStruct((b, MLA_HEADS, s, LANES), BF16)
    tok_shape = jax.ShapeDtypeStruct((b, s, 256), BF16)
    out_shape = [head_shape, head_shape, jax.ShapeDtypeStruct((b, nt, MLA_OUT, PREP_TM), BF16)] + [tok_shape] * 6
    return pl.pallas_call(
        _prep_kernel,
        grid=(b, nt),
        in_specs=in_specs,
        out_specs=out_specs,
        out_shape=out_shape,
        compiler_params=_cparams(("parallel", "parallel")),
        name="mix_prep",
    )(x3, *weights, p["rope_c"], p["rope_s1"], p["rope_s2"], *tail)


def _mla_kernel(q_ref, k_ref, vt_ref, o_ref, s_scr, p_scr, acc_scr, *, nkb):
    tq = q_ref.shape[2]

    def scores(n, slot):
        row0 = pl.multiple_of(n * MLA_TK, MLA_TK)
        block_max = []
        for hh in range(2):
            s = _dot_nt(k_ref[0, hh, pl.ds(row0, MLA_TK), :], q_ref[0, hh])
            s_scr[slot, hh] = s
            block_max.append(jnp.max(s, axis=0, keepdims=True))
        return tuple(block_max)

    def softmax(slot, block_max, m, l):
        m_out, l_out, alpha_out = [], [], []
        for hh in range(2):
            m_new = jnp.maximum(m[hh], block_max[hh])
            alpha = jnp.exp2(m[hh] - m_new)
            p = jnp.exp2(s_scr[slot, hh] - m_new)
            p_scr[slot, hh] = p.astype(BF16)
            m_out.append(m_new)
            l_out.append(alpha * l[hh] + jnp.sum(p, axis=0, keepdims=True))
            alpha_out.append(alpha)
        return tuple(m_out), tuple(l_out), tuple(alpha_out)

    def accumulate(n, slot, alpha):
        for hh in range(2):
            vt = vt_ref[0, n, hh * MLA_V:(hh + 1) * MLA_V, :]
            acc_scr[hh] = alpha[hh] * acc_scr[hh] + _dot(vt, p_scr[slot, hh])

    def step(n, slot, carry):
        m, l, block_max, alpha = carry
        accumulate(n - 2, slot, alpha)
        new_max = scores(n, slot)
        m, l, alpha = softmax(1 - slot, block_max, m, l)
        return m, l, new_max, alpha

    acc_scr[...] = jnp.zeros_like(acc_scr)
    m = (jnp.full((1, tq), NEG_INF, F32),) * 2
    l = (jnp.zeros((1, tq), F32),) * 2
    bmax = scores(0, 0)
    bmax_next = scores(1, 1)
    m, l, alpha = softmax(0, bmax, m, l)
    carry = (m, l, bmax_next, alpha)

    def pair(t, carry):
        carry = step(2 * t + 2, 0, carry)
        return step(2 * t + 3, 1, carry)

    m, l, bmax, alpha = lax.fori_loop(0, (nkb - 2) // 2, pair, carry)
    accumulate(nkb - 2, 0, alpha)
    m, l, alpha = softmax(1, bmax, m, l)
    accumulate(nkb - 1, 1, alpha)
    o_t = jnp.concatenate([acc_scr[hh] / l[hh] for hh in range(2)], axis=0)
    o_ref[0] = o_t.T


def _mla_attn(q, k, vt):
    b, h, s, _ = q.shape
    nkb = s // MLA_TK
    assert nkb % 2 == 0 and nkb >= 4
    return pl.pallas_call(
        functools.partial(_mla_kernel, nkb=nkb),
        grid=(b, h // 2, s // MLA_TQ),
        in_specs=[
            pl.BlockSpec((1, 2, MLA_TQ, LANES), lambda bi, hp, i: (bi, hp, i, 0)),
            pl.BlockSpec((1, 2, s, LANES), lambda bi, hp, i: (bi, hp, 0, 0)),
            pl.BlockSpec((1, nkb, 2 * MLA_V, MLA_TK), lambda bi, hp, i: (bi, 0, hp, 0)),
        ],
        out_specs=pl.BlockSpec((1, MLA_TQ, 2 * MLA_V), lambda bi, hp, i: (bi, i, hp)),
        out_shape=jax.ShapeDtypeStruct((b, s, MLA_OUT), F32),
        scratch_shapes=[pltpu.VMEM((2, 2, MLA_TK, MLA_TQ), F32),
                        pltpu.VMEM((2, 2, MLA_TK, MLA_TQ), BF16),
                        pltpu.VMEM((2, MLA_V, MLA_TQ), F32)],
        compiler_params=_cparams(("parallel", "parallel", "arbitrary")),
        name="mla_attn",
    )(q, k, vt)


def _lane_head_mask(h):
    lane = lax.broadcasted_iota(jnp.int32, (1, 256), 1)
    return (lane >= h * 64) & (lane < (h + 1) * 64)


def _window_attn(q, kw, vw, bias_ref, sinks):
    o = jnp.zeros((q.shape[0], 256), F32)
    for h in range(4):
        mask = _lane_head_mask(h)
        s = _dot_nt(jnp.where(mask, q, jnp.zeros_like(q)), kw) + bias_ref[0, h]
        m = jnp.max(s, axis=-1, keepdims=True)
        if sinks is not None:
            m = jnp.maximum(m, sinks[h])
        p = jnp.exp(s - m)
        l = jnp.sum(p, axis=-1, keepdims=True)
        if sinks is not None:
            l = l + jnp.exp(sinks[h] - m)
        o = o + _dot((p / l).astype(BF16), jnp.where(mask, vw, jnp.zeros_like(vw)))
    return o


def _swa_kernel(sink_ref, q_ref, k_ref, v_ref, bias_ref, g_ref, o_ref, *, seq):
    i = pl.program_id(1)
    w0 = pl.multiple_of(jnp.clip(i * SWA_TQ - SWA_WINDOW, 0, seq - SWA_WIN), SWA_WINDOW)
    kw = k_ref[0, pl.ds(w0, SWA_WIN), :]
    vw = v_ref[0, pl.ds(w0, SWA_WIN), :]
    sinks = [sink_ref[h] for h in range(SWA_HEADS)]
    o = _window_attn(q_ref[0], kw, vw, bias_ref, sinks)
    o_ref[0] = _rms(o, g_ref[...]).astype(BF16)


def _na_kernel(q_ref, k_ref, v_ref, bias_ref, g_ref, o_ref, *, rows):
    i = pl.program_id(1)
    r0 = jnp.clip(i * (NA_TQ // GRID_W) - NA_KR // 2, 0, rows - NA_WROWS)
    w0 = pl.multiple_of(r0 * GRID_W, GRID_W)
    kw = k_ref[0, pl.ds(w0, NA_WIN), :]
    vw = v_ref[0, pl.ds(w0, NA_WIN), :]
    o = _window_attn(q_ref[0], kw, vw, bias_ref, None)
    o_ref[0] = _rms(o, g_ref[...]).astype(BF16)


def _edge_pattern(i, n):
    return (i > 0).astype(jnp.int32) + (i == n - 1).astype(jnp.int32)


def _swa_attn(q, k, v, bias, sink, g):
    b, s, _ = q.shape
    nq = s // SWA_TQ
    return pl.pallas_call(
        functools.partial(_swa_kernel, seq=s),
        grid=(b, nq),
        in_specs=[
            pl.BlockSpec(memory_space=pltpu.SMEM),
            pl.BlockSpec((1, SWA_TQ, 256), lambda bi, i: (bi, i, 0)),
            pl.BlockSpec((1, s, 256), lambda bi, i: (bi, 0, 0)),
            pl.BlockSpec((1, s, 256), lambda bi, i: (bi, 0, 0)),
            pl.BlockSpec((1, SWA_HEADS, SWA_TQ, SWA_WIN), lambda bi, i: (_edge_pattern(i, nq), 0, 0, 0)),
            _resident((1, SWA_OUT)),
        ],
        out_specs=pl.BlockSpec((1, SWA_TQ, 256), lambda bi, i: (bi, i, 0)),
        out_shape=jax.ShapeDtypeStruct((b, s, SWA_OUT), BF16),
        compiler_params=_cparams(("parallel", "arbitrary")),
        name="swa_attn",
    )(sink, q, k, v, bias, g)


def _na_attn(q, k, v, bias, g):
    b, s, _ = q.shape
    nq = s // NA_TQ
    return pl.pallas_call(
        functools.partial(_na_kernel, rows=s // GRID_W),
        grid=(b, nq),
        in_specs=[
            pl.BlockSpec((1, NA_TQ, 256), lambda bi, i: (bi, i, 0)),
            pl.BlockSpec((1, s, 256), lambda bi, i: (bi, 0, 0)),
            pl.BlockSpec((1, s, 256), lambda bi, i: (bi, 0, 0)),
            pl.BlockSpec((1, NA_HEADS, NA_TQ, NA_WIN), lambda bi, i: (_edge_pattern(i, nq), 0, 0, 0)),
            _resident((1, NA_OUT)),
        ],
        out_specs=pl.BlockSpec((1, NA_TQ, 256), lambda bi, i: (bi, i, 0)),
        out_shape=jax.ShapeDtypeStruct((b, s, NA_OUT), BF16),
        compiler_params=_cparams(("parallel", "arbitrary")),
        name="na_attn",
    )(q, k, v, bias, g)


def _swa_bias_table(seq):
    slopes = 2.0 ** (-8.0 * np.arange(1, SWA_HEADS + 1, dtype=np.float64) / SWA_HEADS)
    ql = np.arange(SWA_TQ)[:, None]
    kl = np.arange(SWA_WIN)[None, :]
    pats = []
    for q0, w0 in ((0, 0), (SWA_TQ, SWA_TQ - SWA_WINDOW), (seq - SWA_TQ, seq - SWA_WIN)):
        dist = np.abs(ql + q0 - (kl + w0))
        pats.append(np.where(dist[None] <= SWA_WINDOW, -slopes[:, None, None] * dist[None], NEG_INF))
    return jnp.asarray(np.stack(pats), F32)


def _na_bias_table(rel_bias, rows):
    qr_l = np.arange(NA_TQ // GRID_W)
    kr_l = np.arange(NA_WROWS)
    qc = np.arange(GRID_W)[:, None]
    kc = np.arange(GRID_W)[None, :]
    c0 = np.clip(qc - NA_KC // 2, 0, GRID_W - NA_KC)
    col_ok = (kc >= c0) & (kc < c0 + NA_KC)
    dc = np.clip(kc - qc + (NA_KC - 1), 0, 2 * NA_KC - 2)
    pick_dc = (dc[None] == np.arange(2 * NA_KC - 1)[:, None, None]).astype(np.float32)
    last_first = rows - NA_TQ // GRID_W
    pick_dr, ok = [], []
    for r_first, w_row0 in ((0, 0), (NA_KR // 2, 0), (last_first, rows - NA_WROWS)):
        rq = r_first + qr_l[:, None]
        rk = w_row0 + kr_l[None, :]
        r0 = np.clip(rq - NA_KR // 2, 0, rows - NA_KR)
        row_ok = (rk >= r0) & (rk < r0 + NA_KR)
        dr = np.clip(rk - rq + (NA_KR - 1), 0, 2 * NA_KR - 2)
        pick_dr.append((dr[None] == np.arange(2 * NA_KR - 1)[:, None, None]).astype(np.float32))
        ok.append(row_ok[:, None, :, None] & col_ok[None, :, None, :])
    hi = lax.Precision.HIGHEST
    t = jnp.einsum("hdc,pdqk->phqkc", rel_bias, jnp.asarray(np.stack(pick_dr)), precision=hi)
    t = jnp.einsum("phqkc,cxy->phqxky", t, jnp.asarray(pick_dc), precision=hi)
    t = jnp.where(jnp.asarray(np.stack(ok))[:, None], t, NEG_INF)
    return t.reshape(3, NA_HEADS, NA_TQ, NA_WIN).astype(F32)


def _outproj_kernel(x_ref, om_ref, os_ref, on_ref, g_ref, w_ref, o_ref):
    om = _rms(om_ref[...], g_ref[...]).astype(BF16)
    cat = jnp.concatenate([om, os_ref[...], on_ref[...]], axis=-1)
    o_ref[...] = x_ref[...] + _dot(cat, w_ref[...])


def _out_proj(x2, o_mla, o_swa, o_na, g_mla, w):
    n = x2.shape[0]
    row = lambda w_: pl.BlockSpec((OUT_TM, w_), lambda i: (i, 0))
    return pl.pallas_call(
        _outproj_kernel,
        grid=(n // OUT_TM,),
        in_specs=[row(D_MODEL), row(MLA_OUT), row(SWA_OUT), row(NA_OUT), _resident((1, MLA_OUT)),
                  _resident(w.shape)],
        out_specs=row(D_MODEL),
        out_shape=jax.ShapeDtypeStruct((n, D_MODEL), F32),
        compiler_params=_cparams(("parallel",)),
        name="out_proj",
    )(x2, o_mla, o_swa, o_na, g_mla, w)


def _memkv_kernel(mem_ref, g_ref, w_ref, kg_ref, bd_ref, k_ref, v_ref):
    m = _rms(mem_ref[0], g_ref[...]).astype(BF16)
    kv = _dot(m, w_ref[...])
    k = kv[:, :256]
    ss = _dot((k * k).astype(BF16), bd_ref[...])
    k = (k * lax.rsqrt(ss * (1.0 / MEM_DH) + EPS) * kg_ref[...]).astype(BF16)
    v = kv[:, 256:].astype(BF16)
    for h in range(MEM_HEADS):
        mask = _lane_head_mask(h)
        k_ref[0, h] = jnp.where(mask, k, jnp.zeros_like(k))
        v_ref[0, h] = jnp.where(mask, v, jnp.zeros_like(v))


def _mem_kv(mem, g, w, kg, bd):
    b, m, _ = mem.shape
    out = jax.ShapeDtypeStruct((b, MEM_HEADS, m, 256), BF16)
    spec = pl.BlockSpec((1, MEM_HEADS, m, 256), lambda bi: (bi, 0, 0, 0))
    return pl.pallas_call(
        _memkv_kernel,
        grid=(b,),
        in_specs=[pl.BlockSpec((1, m, D_MODEL), lambda bi: (bi, 0, 0)), _resident(g.shape), _resident(w.shape),
                  _resident(kg.shape), _resident(bd.shape)],
        out_specs=[spec, spec],
        out_shape=[out, out],
        compiler_params=_cparams(("parallel",)),
        name="mem_kv",
    )(mem, g, w, kg, bd)


def _memattn_kernel(x_ref, g_ref, wq_ref, qg_ref, bd_ref, k_ref, v_ref, wo_ref, o_ref):
    x = x_ref[0]
    xn = _rms(x, g_ref[...]).astype(BF16)
    q = _dot(xn, wq_ref[...])
    ss = _dot((q * q).astype(BF16), bd_ref[...])
    q = (q * lax.rsqrt(ss * (1.0 / MEM_DH) + EPS) * qg_ref[...] * (MEM_DH ** -0.5)).astype(BF16)
    o = jnp.zeros((x.shape[0], 256), F32)
    for h in range(MEM_HEADS):
        s = _dot_nt(q, k_ref[0, h])
        m = jnp.max(s, axis=-1, keepdims=True)
        p = jnp.exp(s - m)
        l = jnp.sum(p, axis=-1, keepdims=True)
        o = o + _dot((p / l).astype(BF16), v_ref[0, h])
    o_ref[0] = x + _dot(o.astype(BF16), wo_ref[...])


def _mem_attn(x3, g, wq, qg, bd, k, v, wo):
    b, s, _ = x3.shape
    m = k.shape[2]
    tok = pl.BlockSpec((1, MEM_TM, D_MODEL), lambda bi, i: (bi, i, 0))
    kv = pl.BlockSpec((1, MEM_HEADS, m, 256), lambda bi, i: (bi, 0, 0, 0))
    return pl.pallas_call(
        _memattn_kernel,
        grid=(b, s // MEM_TM),
        in_specs=[tok, _resident(g.shape), _resident(wq.shape), _resident(qg.shape), _resident(bd.shape), kv, kv,
                  _resident(wo.shape)],
        out_specs=tok,
        out_shape=jax.ShapeDtypeStruct((b, s, D_MODEL), F32),
        compiler_params=_cparams(("parallel", "parallel")),
        name="mem_attn",
    )(x3, g, wq, qg, bd, k, v, wo)


def _rope_tables(seq):
    half = MLA_ROPE // 2
    inv = 1.0 / (ROPE_THETA ** (jnp.arange(0, MLA_ROPE, 2, dtype=F32) / MLA_ROPE))
    ang = jnp.arange(seq, dtype=F32)[:, None] * inv[None, :]
    cos, sin = jnp.cos(ang), jnp.sin(ang)
    zeros = lambda w: jnp.zeros((seq, w), F32)
    pad = LANES - MLA_QK
    rc = jnp.concatenate([jnp.ones((seq, MLA_NOPE), F32), cos, cos, zeros(pad)], axis=1)
    rs1 = jnp.concatenate([zeros(MLA_NOPE), -sin, zeros(half), zeros(pad)], axis=1)
    rs2 = jnp.concatenate([zeros(MLA_NOPE), zeros(half), sin, zeros(pad)], axis=1)
    return rc, rs1, rs2


def _pad_lanes(a, width=LANES):
    return jnp.pad(a, [(0, 0)] * (a.ndim - 1) + [(0, width - a.shape[-1])])


def _layer_params(l, seq, a):
    row = lambda v: v[l].reshape(1, -1).astype(F32)
    p = {}
    for name in ("ffn1", "ffn2"):
        w_in = a[name + "_w_in"][l]
        p[name] = (row(a[name + "_norm"]), w_in[:, :D_FF].astype(BF16), w_in[:, D_FF:].astype(BF16),
                   a[name + "_w_out"][l].astype(BF16))
    w = a["w_mix_in"][l]
    o_ckv = MLA_Q_LORA
    o_kr = o_ckv + MLA_KV_LORA
    o_swa = o_kr + MLA_ROPE
    o_na = o_swa + (SWA_HEADS + 2 * SWA_KV_HEADS) * SWA_DH
    w_kr = jnp.pad(w[:, o_kr:o_swa], ((0, 0), (MLA_NOPE, LANES - MLA_QK)))
    p["w_mla"] = jnp.concatenate([w[:, :o_kr], w_kr], axis=1).astype(BF16)
    p["mix_norm"] = row(a["mix_norm"])
    p["mla_q_norm"] = row(a["mla_q_norm"])
    p["mla_kv_norm"] = row(a["mla_kv_norm"])
    w_uq = a["mla_w_uq"][l].reshape(MLA_Q_LORA, MLA_HEADS, MLA_QK)
    p["w_uq"] = _pad_lanes(w_uq).reshape(MLA_Q_LORA, MLA_HEADS * LANES).astype(BF16)
    w_ukv = a["mla_w_ukv"][l].reshape(MLA_KV_LORA, MLA_HEADS, MLA_NOPE + MLA_V)
    p["w_uk"] = _pad_lanes(w_ukv[:, :, :MLA_NOPE]).reshape(MLA_KV_LORA, MLA_HEADS * LANES).astype(BF16)
    p["w_uvt"] = w_ukv[:, :, MLA_NOPE:].reshape(MLA_KV_LORA, MLA_OUT).T.astype(BF16)
    p["mla_q_gain"] = _pad_lanes(row(a["mla_q_gain"]))
    p["mla_k_gain"] = _pad_lanes(row(a["mla_k_gain"]))
    p["rope_c"], p["rope_s1"], p["rope_s2"] = _rope_tables(seq)
    w_sq = w[:, o_swa:o_swa + SWA_OUT]
    w_sk = w[:, o_swa + SWA_OUT:o_swa + SWA_OUT + SWA_KV_HEADS * SWA_DH].reshape(D_MODEL, SWA_KV_HEADS, SWA_DH)
    w_sv = w[:, o_swa + SWA_OUT + SWA_KV_HEADS * SWA_DH:o_na].reshape(D_MODEL, SWA_KV_HEADS, SWA_DH)
    expand = lambda t: jnp.repeat(t, SWA_GROUP, axis=1).reshape(D_MODEL, SWA_OUT)
    p["w_swa"] = jnp.concatenate([w_sq, expand(w_sk), expand(w_sv)], axis=1).astype(BF16)
    p["swa_q_gain"] = jnp.tile(row(a["swa_q_gain"]), (1, SWA_HEADS))
    p["swa_k_gain"] = jnp.tile(row(a["swa_k_gain"]), (1, SWA_HEADS))
    p["swa_sink"] = a["swa_sink"][l].astype(F32)
    p["w_na"] = w[:, o_na:].astype(BF16)
    p["na_q_gain"] = jnp.tile(row(a["na_q_gain"]), (1, NA_HEADS))
    p["na_k_gain"] = jnp.tile(row(a["na_k_gain"]), (1, NA_HEADS))
    p["na_bias"] = _na_bias_table(a["na_rel_bias"][l].astype(F32), seq // GRID_W)
    g = row(a["grp_out_gain"])
    p["g_mla"], p["g_swa"], p["g_na"] = g[:, :MLA_OUT], g[:, MLA_OUT:MLA_OUT + SWA_OUT], g[:, MLA_OUT + SWA_OUT:]
    p["w_mix_out"] = a["w_mix_out"][l].astype(BF16)
    p["mem_norm_x"] = row(a["mem_norm_x"])
    p["mem_norm_m"] = row(a["mem_norm_m"])
    p["mem_w_q"] = a["mem_w_q"][l].astype(BF16)
    p["mem_w_kv"] = a["mem_w_kv"][l].astype(BF16)
    p["mem_q_gain"] = jnp.tile(row(a["mem_q_gain"]), (1, MEM_HEADS))
    p["mem_k_gain"] = jnp.tile(row(a["mem_k_gain"]), (1, MEM_HEADS))
    p["mem_w_o"] = a["mem_w_o"][l].astype(BF16)
    p["block_norm"] = row(a["block_norm"])
    seg = np.arange(256) // 64
    p["bd"] = jnp.asarray(seg[:, None] == seg[None, :], BF16)
    return p


def kernel(x, mem, ffn1_norm, ffn1_w_in, ffn1_w_out, mix_norm, w_mix_in, mla_q_norm, mla_w_uq, mla_kv_norm, mla_w_ukv, mla_q_gain, mla_k_gain, swa_q_gain, swa_k_gain, swa_sink, na_q_gain, na_k_gain, na_rel_bias, grp_out_gain, w_mix_out, mem_norm_x, mem_norm_m, mem_w_q, mem_w_kv, mem_q_gain, mem_k_gain, mem_w_o, ffn2_norm, ffn2_w_in, ffn2_w_out, block_norm):
    a = dict(ffn1_norm=ffn1_norm, ffn1_w_in=ffn1_w_in, ffn1_w_out=ffn1_w_out, mix_norm=mix_norm, w_mix_in=w_mix_in,
             mla_q_norm=mla_q_norm, mla_w_uq=mla_w_uq, mla_kv_norm=mla_kv_norm, mla_w_ukv=mla_w_ukv,
             mla_q_gain=mla_q_gain, mla_k_gain=mla_k_gain, swa_q_gain=swa_q_gain, swa_k_gain=swa_k_gain,
             swa_sink=swa_sink, na_q_gain=na_q_gain, na_k_gain=na_k_gain, na_rel_bias=na_rel_bias,
             grp_out_gain=grp_out_gain, w_mix_out=w_mix_out, mem_norm_x=mem_norm_x, mem_norm_m=mem_norm_m,
             mem_w_q=mem_w_q, mem_w_kv=mem_w_kv, mem_q_gain=mem_q_gain, mem_k_gain=mem_k_gain, mem_w_o=mem_w_o,
             ffn2_norm=ffn2_norm, ffn2_w_in=ffn2_w_in, ffn2_w_out=ffn2_w_out, block_norm=block_norm)
    b, s, d = x.shape
    depth = ffn1_norm.shape[0]
    swa_bias = _swa_bias_table(s)
    x = x.astype(F32)
    mem = mem.astype(F32)
    for l in range(depth):
        p = _layer_params(l, s, a)
        x2 = _ffn(x.reshape(b * s, d), *p["ffn1"])
        q, k, vt, sq, sk, sv, nq, nk, nv = _prep(x2.reshape(b, s, d), p)
        o_mla = _mla_attn(q, k, vt)
        o_swa = _swa_attn(sq, sk, sv, swa_bias, p["swa_sink"], p["g_swa"])
        o_na = _na_attn(nq, nk, nv, p["na_bias"], p["g_na"])
        x2 = _out_proj(x2, o_mla.reshape(b * s, MLA_OUT), o_swa.reshape(b * s, SWA_OUT),
                       o_na.reshape(b * s, NA_OUT), p["g_mla"], p["w_mix_out"])
        mk, mv = _mem_kv(mem, p["mem_norm_m"], p["mem_w_kv"], p["mem_k_gain"], p["bd"])
        x3 = _mem_attn(x2.reshape(b, s, d), p["mem_norm_x"], p["mem_w_q"], p["mem_q_gain"], p["bd"], mk, mv,
                       p["mem_w_o"])
        x = _ffn(x3.reshape(b * s, d), *p["ffn2"], final_g=p["block_norm"])
    return x.reshape(b, s, d)
```

```python
import functools

import numpy as np
import jax
import jax.numpy as jnp
from jax import lax
from jax.experimental import pallas as pl
from jax.experimental.pallas import tpu as pltpu

F32 = jnp.float32
BF16 = jnp.bfloat16

D_MODEL = 1024
D_FF = 2816
N_MEM = 256
GRID_W = 64
EPS = 1e-6
NEG_INF = -1e30
LOG2_E = 1.4426950408889634

MLA_HEADS = 8
MLA_Q_LORA = 256
MLA_KV_LORA = 128
MLA_NOPE = 64
MLA_ROPE = 32
MLA_QK = MLA_NOPE + MLA_ROPE
MLA_V = 64
ROPE_THETA = 10000.0
MLA_OUT = MLA_HEADS * MLA_V

SWA_HEADS = 4
SWA_KV_HEADS = 2
SWA_GROUP = SWA_HEADS // SWA_KV_HEADS
SWA_DH = 64
SWA_WINDOW = 128
SWA_OUT = SWA_HEADS * SWA_DH

NA_HEADS = 4
NA_DH = 64
NA_KR = 8
NA_KC = 16
NA_OUT = NA_HEADS * NA_DH

MEM_HEADS = 4
MEM_DH = 64

LANES = 128
V7X_VMEM_BYTES = 64 * 2**20
VMEM_LIMIT = 56 * 2**20

FFN_TM = 512
PREP_TM = 512
MLA_TQ = 512
MLA_TK = 512
MLA_HG = 4
MLA_ACC_ROWS = MLA_V + 16
SWA_TQ = 256
SWA_WIN = SWA_TQ + 2 * SWA_WINDOW
NA_TQ = 4 * GRID_W
NA_WROWS = 12
NA_WIN = NA_WROWS * GRID_W
OUT_TM = 512
MEM_TM = 512


def _cparams(sem):
    return pltpu.CompilerParams(dimension_semantics=sem, vmem_limit_bytes=VMEM_LIMIT)


def _rms(x, g):
    ms = jnp.mean(x * x, axis=-1, keepdims=True)
    return x * lax.rsqrt(ms + EPS) * g


def _dot(a, b):
    return jnp.dot(a, b, preferred_element_type=F32)


def _dot_nt(a, b):
    return lax.dot_general(a, b, (((1,), (1,)), ((), ())), preferred_element_type=F32)


def _resident(shape):
    nd = len(shape)
    return pl.BlockSpec(shape, lambda *_: (0,) * nd)


def _ffn_kernel(x_ref, g_ref, wg_ref, wu_ref, wo_ref, *rest, final_norm):
    if final_norm:
        fg_ref, o_ref = rest
    else:
        (o_ref,) = rest
    x = x_ref[...]
    xn = _rms(x, g_ref[...]).astype(BF16)
    g = _dot(xn, wg_ref[...])
    u = _dot(xn, wu_ref[...])
    a = (g * jax.nn.sigmoid(g) * u).astype(BF16)
    out = x + 0.5 * _dot(a, wo_ref[...])
    if final_norm:
        out = _rms(out, fg_ref[...])
    o_ref[...] = out


def _ffn(x2, g, wg, wu, wo, final_g=None):
    n = x2.shape[0]
    single = pl.Buffered(1)
    in_specs = [
        pl.BlockSpec((FFN_TM, D_MODEL), lambda i: (i, 0)),
        _resident((1, D_MODEL)),
        pl.BlockSpec((D_MODEL, D_FF), lambda i: (0, 0), pipeline_mode=single),
        pl.BlockSpec((D_MODEL, D_FF), lambda i: (0, 0), pipeline_mode=single),
        pl.BlockSpec((D_FF, D_MODEL), lambda i: (0, 0), pipeline_mode=single),
    ]
    args = [x2, g, wg, wu, wo]
    if final_g is not None:
        in_specs.append(_resident((1, D_MODEL)))
        args.append(final_g)
    return pl.pallas_call(
        functools.partial(_ffn_kernel, final_norm=final_g is not None),
        grid=(n // FFN_TM,),
        in_specs=in_specs,
        out_specs=pl.BlockSpec((FFN_TM, D_MODEL), lambda i: (i, 0)),
        out_shape=jax.ShapeDtypeStruct((n, D_MODEL), F32),
        compiler_params=_cparams(("parallel",)),
        name="ffn",
    )(*args)


def _prep_kernel(x_ref, mixg_ref, wmla_ref, qng_ref, kvng_ref, wuq_ref, wuk_ref, wuvt_ref,
                 qgain_ref, kgain_ref, rc_ref, rs1_ref, rs2_ref,
                 wswa_ref, sqg_ref, skg_ref, wna_ref, nqg_ref, nkg_ref, bd_ref,
                 q_ref, k_ref, vt_ref, sq_ref, sk_ref, sv_ref, nq_ref, nk_ref, nv_ref):
    xn = _rms(x_ref[0], mixg_ref[...]).astype(BF16)

    z = _dot(xn, wmla_ref[...])
    cqn = _rms(z[:, :MLA_Q_LORA], qng_ref[...]).astype(BF16)
    ckvn = _rms(z[:, MLA_Q_LORA:MLA_Q_LORA + MLA_KV_LORA], kvng_ref[...]).astype(BF16)
    kr = z[:, MLA_Q_LORA + MLA_KV_LORA:]
    qa = _dot(cqn, wuq_ref[...])
    ka = _dot(ckvn, wuk_ref[...])
    vt_ref[0, 0] = _dot_nt(wuvt_ref[...], ckvn).astype(BF16)

    rc, rs1, rs2 = rc_ref[...], rs1_ref[...], rs2_ref[...]
    qgain, kgain = qgain_ref[...], kgain_ref[...]

    def norm_rope(t, gain):
        ss = jnp.sum(t * t, axis=-1, keepdims=True)
        tn = t * lax.rsqrt(ss * (1.0 / MLA_QK) + EPS) * gain
        return tn * rc + pltpu.roll(tn, LANES - MLA_ROPE // 2, 1) * rs1 + pltpu.roll(tn, MLA_ROPE // 2, 1) * rs2

    for h in range(MLA_HEADS):
        sl = slice(h * LANES, (h + 1) * LANES)
        q_ref[0, h] = (norm_rope(qa[:, sl], qgain) * (MLA_QK ** -0.5 * LOG2_E)).astype(BF16)
        k_ref[0, h] = norm_rope(ka[:, sl] + kr, kgain).astype(BF16)

    bd = bd_ref[...]

    def head_norm(t, gain):
        ss = _dot((t * t).astype(BF16), bd)
        return t * lax.rsqrt(ss * (1.0 / SWA_DH) + EPS) * gain

    zs = _dot(xn, wswa_ref[...])
    sq_ref[0] = (head_norm(zs[:, :256], sqg_ref[...]) * (SWA_DH ** -0.5)).astype(BF16)
    sk_ref[0] = head_norm(zs[:, 256:512], skg_ref[...]).astype(BF16)
    sv_ref[0] = zs[:, 512:].astype(BF16)

    zn = _dot(xn, wna_ref[...])
    nq_ref[0] = (head_norm(zn[:, :256], nqg_ref[...]) * (NA_DH ** -0.5)).astype(BF16)
    nk_ref[0] = head_norm(zn[:, 256:512], nkg_ref[...]).astype(BF16)
    nv_ref[0] = zn[:, 512:].astype(BF16)


def _prep(x3, p):
    b, s, _ = x3.shape
    nt = s // PREP_TM
    tok = lambda w: pl.BlockSpec((1, PREP_TM, w), lambda bi, i: (bi, i, 0))
    rope = pl.BlockSpec((PREP_TM, LANES), lambda bi, i: (i, 0))
    weights = [p["mix_norm"], p["w_mla"], p["mla_q_norm"], p["mla_kv_norm"], p["w_uq"], p["w_uk"], p["w_uvt"],
               p["mla_q_gain"], p["mla_k_gain"]]
    tail = [p["w_swa"], p["swa_q_gain"], p["swa_k_gain"], p["w_na"], p["na_q_gain"], p["na_k_gain"], p["bd"]]
    in_specs = ([tok(D_MODEL)] + [_resident(w.shape) for w in weights] + [rope, rope, rope]
                + [_resident(w.shape) for w in tail])
    head_spec = pl.BlockSpec((1, MLA_HEADS, PREP_TM, LANES), lambda bi, i: (bi, 0, i, 0))
    out_specs = [head_spec, head_spec,
                 pl.BlockSpec((1, 1, MLA_OUT, PREP_TM), lambda bi, i: (bi, i, 0, 0))] + [tok(256)] * 6
    head_shape = jax.ShapeDtypeStruct((b, MLA_HEADS, s, LANES), BF16)
    tok_shape = jax.ShapeDtypeStruct((b, s, 256), BF16)
    out_shape = [head_shape, head_shape, jax.ShapeDtypeStruct((b, nt, MLA_OUT, PREP_TM), BF16)] + [tok_shape] * 6
    return pl.pallas_call(
        _prep_kernel,
        grid=(b, nt),
        in_specs=in_specs,
        out_specs=out_specs,
        out_shape=out_shape,
        compiler_params=_cparams(("parallel", "parallel")),
        name="mix_prep",
    )(x3, *weights, p["rope_c"], p["rope_s1"], p["rope_s2"], *tail)


def _mla_kernel(q_ref, k_ref, vt_ref, o_ref, s_scr, p_scr, bmax_scr, alpha_scr, acc_scr, m_scr, *, nkb):
    heads = range(MLA_HG)
    ones_rows = jnp.ones((MLA_ACC_ROWS - MLA_V, MLA_TK), BF16)

    def scores(n, slot, hs=heads):
        row0 = pl.multiple_of(n * MLA_TK, MLA_TK)
        for hh in hs:
            s = _dot_nt(k_ref[0, hh, pl.ds(row0, MLA_TK), :], q_ref[0, hh])
            s_scr[slot, hh] = s
            bmax_scr[slot, hh] = jnp.max(s, axis=0, keepdims=True)

    def softmax(slot, hs=heads):
        for hh in hs:
            m_old = m_scr[hh]
            m_new = jnp.maximum(m_old, bmax_scr[slot, hh])
            m_scr[hh] = m_new
            alpha_scr[slot, hh] = jnp.exp2(m_old - m_new)
            p_scr[slot, hh] = jnp.exp2(s_scr[slot, hh] - m_new).astype(BF16)

    def accumulate(n, slot, hs=heads):
        for hh in hs:
            vt1 = jnp.concatenate([vt_ref[0, n, hh * MLA_V:(hh + 1) * MLA_V, :], ones_rows], axis=0)
            acc_scr[hh] = alpha_scr[slot, hh] * acc_scr[hh] + _dot(vt1, p_scr[slot, hh])

    def step(n, slot):
        for hh in heads:
            scores(n, slot, (hh,))
            softmax(1 - slot, (hh,))
            accumulate(n - 1, 1 - slot, (hh,))

    acc_scr[...] = jnp.zeros_like(acc_scr)
    m_scr[...] = jnp.full_like(m_scr, NEG_INF)
    scores(0, 0)

    def trip(n, carry):
        for slot in range(2):
            pl.when(jnp.bitwise_and(n, 1) == slot)(functools.partial(step, n, slot))
        return carry

    lax.fori_loop(1, nkb, trip, 0)
    for hh in heads:
        softmax(1, (hh,))
        accumulate(nkb - 1, 1, (hh,))
    for hp in range(MLA_HG // 2):
        pair_t = jnp.concatenate(
            [acc_scr[hh, :MLA_V] / acc_scr[hh, MLA_V:MLA_V + 1] for hh in (2 * hp, 2 * hp + 1)], axis=0)
        o_ref[0, :, hp * LANES:(hp + 1) * LANES] = pair_t.T


def _mla_attn(q, k, vt):
    b, h, s, _ = q.shape
    nkb = s // MLA_TK
    assert nkb % 2 == 0 and nkb >= 4 and h % MLA_HG == 0
    return pl.pallas_call(
        functools.partial(_mla_kernel, nkb=nkb),
        grid=(b, h // MLA_HG, s // MLA_TQ),
        in_specs=[
            pl.BlockSpec((1, MLA_HG, MLA_TQ, LANES), lambda bi, hg, i: (bi, hg, i, 0)),
            pl.BlockSpec((1, MLA_HG, s, LANES), lambda bi, hg, i: (bi, hg, 0, 0)),
            pl.BlockSpec((1, nkb, MLA_HG * MLA_V, MLA_TK), lambda bi, hg, i: (bi, 0, hg, 0)),
        ],
        out_specs=pl.BlockSpec((1, MLA_TQ, MLA_HG * MLA_V), lambda bi, hg, i: (bi, i, hg)),
        out_shape=jax.ShapeDtypeStruct((b, s, MLA_OUT), F32),
        scratch_shapes=[pltpu.VMEM((2, MLA_HG, MLA_TK, MLA_TQ), F32),
                        pltpu.VMEM((2, MLA_HG, MLA_TK, MLA_TQ), BF16),
                        pltpu.VMEM((2, MLA_HG, 1, MLA_TQ), F32),
                        pltpu.VMEM((2, MLA_HG, 1, MLA_TQ), F32),
                        pltpu.VMEM((MLA_HG, MLA_ACC_ROWS, MLA_TQ), F32),
                        pltpu.VMEM((MLA_HG, 1, MLA_TQ), F32)],
        compiler_params=_cparams(("parallel", "parallel", "arbitrary")),
        name="mla_attn",
    )(q, k, vt)


def _lane_head_mask(h):
    lane = lax.broadcasted_iota(jnp.int32, (1, 256), 1)
    return (lane >= h * 64) & (lane < (h + 1) * 64)


def _window_attn(q, kw, vw, bias_ref, sinks):
    o = jnp.zeros((q.shape[0], 256), F32)
    for h in range(4):
        mask = _lane_head_mask(h)
        s = _dot_nt(jnp.where(mask, q, jnp.zeros_like(q)), kw) + bias_ref[0, h]
        m = jnp.max(s, axis=-1, keepdims=True)
        if sinks is not None:
            m = jnp.maximum(m, sinks[h])
        p = jnp.exp(s - m)
        l = jnp.sum(p, axis=-1, keepdims=True)
        if sinks is not None:
            l = l + jnp.exp(sinks[h] - m)
        o = o + _dot((p / l).astype(BF16), jnp.where(mask, vw, jnp.zeros_like(vw)))
    return o


def _swa_kernel(sink_ref, q_ref, k_ref, v_ref, bias_ref, g_ref, o_ref, *, seq):
    i = pl.program_id(1)
    w0 = pl.multiple_of(jnp.clip(i * SWA_TQ - SWA_WINDOW, 0, seq - SWA_WIN), SWA_WINDOW)
    kw = k_ref[0, pl.ds(w0, SWA_WIN), :]
    vw = v_ref[0, pl.ds(w0, SWA_WIN), :]
    sinks = [sink_ref[h] for h in range(SWA_HEADS)]
    o = _window_attn(q_ref[0], kw, vw, bias_ref, sinks)
    o_ref[0] = _rms(o, g_ref[...]).astype(BF16)


def _na_kernel(q_ref, k_ref, v_ref, bias_ref, g_ref, o_ref, *, rows):
    i = pl.program_id(1)
    r0 = jnp.clip(i * (NA_TQ // GRID_W) - NA_KR // 2, 0, rows - NA_WROWS)
    w0 = pl.multiple_of(r0 * GRID_W, GRID_W)
    kw = k_ref[0, pl.ds(w0, NA_WIN), :]
    vw = v_ref[0, pl.ds(w0, NA_WIN), :]
    o = _window_attn(q_ref[0], kw, vw, bias_ref, None)
    o_ref[0] = _rms(o, g_ref[...]).astype(BF16)


def _edge_pattern(i, n):
    return (i > 0).astype(jnp.int32) + (i == n - 1).astype(jnp.int32)


def _swa_attn(q, k, v, bias, sink, g):
    b, s, _ = q.shape
    nq = s // SWA_TQ
    return pl.pallas_call(
        functools.partial(_swa_kernel, seq=s),
        grid=(b, nq),
        in_specs=[
            pl.BlockSpec(memory_space=pltpu.SMEM),
            pl.BlockSpec((1, SWA_TQ, 256), lambda bi, i: (bi, i, 0)),
            pl.BlockSpec((1, s, 256), lambda bi, i: (bi, 0, 0)),
            pl.BlockSpec((1, s, 256), lambda bi, i: (bi, 0, 0)),
            pl.BlockSpec((1, SWA_HEADS, SWA_TQ, SWA_WIN), lambda bi, i: (_edge_pattern(i, nq), 0, 0, 0)),
            _resident((1, SWA_OUT)),
        ],
        out_specs=pl.BlockSpec((1, SWA_TQ, 256), lambda bi, i: (bi, i, 0)),
        out_shape=jax.ShapeDtypeStruct((b, s, SWA_OUT), BF16),
        compiler_params=_cparams(("parallel", "arbitrary")),
        name="swa_attn",
    )(sink, q, k, v, bias, g)


def _na_attn(q, k, v, bias, g):
    b, s, _ = q.shape
    nq = s // NA_TQ
    return pl.pallas_call(
        functools.partial(_na_kernel, rows=s // GRID_W),
        grid=(b, nq),
        in_specs=[
            pl.BlockSpec((1, NA_TQ, 256), lambda bi, i: (bi, i, 0)),
            pl.BlockSpec((1, s, 256), lambda bi, i: (bi, 0, 0)),
            pl.BlockSpec((1, s, 256), lambda bi, i: (bi, 0, 0)),
            pl.BlockSpec((1, NA_HEADS, NA_TQ, NA_WIN), lambda bi, i: (_edge_pattern(i, nq), 0, 0, 0)),
            _resident((1, NA_OUT)),
        ],
        out_specs=pl.BlockSpec((1, NA_TQ, 256), lambda bi, i: (bi, i, 0)),
        out_shape=jax.ShapeDtypeStruct((b, s, NA_OUT), BF16),
        compiler_params=_cparams(("parallel", "arbitrary")),
        name="na_attn",
    )(q, k, v, bias, g)


def _swa_bias_table(seq):
    slopes = 2.0 ** (-8.0 * np.arange(1, SWA_HEADS + 1, dtype=np.float64) / SWA_HEADS)
    ql = np.arange(SWA_TQ)[:, None]
    kl = np.arange(SWA_WIN)[None, :]
    pats = []
    for q0, w0 in ((0, 0), (SWA_TQ, SWA_TQ - SWA_WINDOW), (seq - SWA_TQ, seq - SWA_WIN)):
        dist = np.abs(ql + q0 - (kl + w0))
        pats.append(np.where(dist[None] <= SWA_WINDOW, -slopes[:, None, None] * dist[None], NEG_INF))
    return jnp.asarray(np.stack(pats), F32)


def _na_bias_table(rel_bias, rows):
    qr_l = np.arange(NA_TQ // GRID_W)
    kr_l = np.arange(NA_WROWS)
    qc = np.arange(GRID_W)[:, None]
    kc = np.arange(GRID_W)[None, :]
    c0 = np.clip(qc - NA_KC // 2, 0, GRID_W - NA_KC)
    col_ok = (kc >= c0) & (kc < c0 + NA_KC)
    dc = np.clip(kc - qc + (NA_KC - 1), 0, 2 * NA_KC - 2)
    pick_dc = (dc[None] == np.arange(2 * NA_KC - 1)[:, None, None]).astype(np.float32)
    last_first = rows - NA_TQ // GRID_W
    pick_dr, ok = [], []
    for r_first, w_row0 in ((0, 0), (NA_KR // 2, 0), (last_first, rows - NA_WROWS)):
        rq = r_first + qr_l[:, None]
        rk = w_row0 + kr_l[None, :]
        r0 = np.clip(rq - NA_KR // 2, 0, rows - NA_KR)
        row_ok = (rk >= r0) & (rk < r0 + NA_KR)
        dr = np.clip(rk - rq + (NA_KR - 1), 0, 2 * NA_KR - 2)
        pick_dr.append((dr[None] == np.arange(2 * NA_KR - 1)[:, None, None]).astype(np.float32))
        ok.append(row_ok[:, None, :, None] & col_ok[None, :, None, :])
    hi = lax.Precision.HIGHEST
    t = jnp.einsum("hdc,pdqk->phqkc", rel_bias, jnp.asarray(np.stack(pick_dr)), precision=hi)
    t = jnp.einsum("phqkc,cxy->phqxky", t, jnp.asarray(pick_dc), precision=hi)
    t = jnp.where(jnp.asarray(np.stack(ok))[:, None], t, NEG_INF)
    return t.reshape(3, NA_HEADS, NA_TQ, NA_WIN).astype(F32)


def _outproj_kernel(x_ref, om_ref, os_ref, on_ref, g_ref, w_ref, o_ref):
    om = _rms(om_ref[...], g_ref[...]).astype(BF16)
    cat = jnp.concatenate([om, os_ref[...], on_ref[...]], axis=-1)
    o_ref[...] = x_ref[...] + _dot(cat, w_ref[...])


def _out_proj(x2, o_mla, o_swa, o_na, g_mla, w):
    n = x2.shape[0]
    row = lambda w_: pl.BlockSpec((OUT_TM, w_), lambda i: (i, 0))
    return pl.pallas_call(
        _outproj_kernel,
        grid=(n // OUT_TM,),
        in_specs=[row(D_MODEL), row(MLA_OUT), row(SWA_OUT), row(NA_OUT), _resident((1, MLA_OUT)),
                  _resident(w.shape)],
        out_specs=row(D_MODEL),
        out_shape=jax.ShapeDtypeStruct((n, D_MODEL), F32),
        compiler_params=_cparams(("parallel",)),
        name="out_proj",
    )(x2, o_mla, o_swa, o_na, g_mla, w)


def _memkv_kernel(mem_ref, g_ref, w_ref, kg_ref, bd_ref, k_ref, v_ref):
    m = _rms(mem_ref[0], g_ref[...]).astype(BF16)
    kv = _dot(m, w_ref[...])
    k = kv[:, :256]
    ss = _dot((k * k).astype(BF16), bd_ref[...])
    k = (k * lax.rsqrt(ss * (1.0 / MEM_DH) + EPS) * kg_ref[...]).astype(BF16)
    v = kv[:, 256:].astype(BF16)
    for h in range(MEM_HEADS):
        mask = _lane_head_mask(h)
        k_ref[0, h] = jnp.where(mask, k, jnp.zeros_like(k))
        v_ref[0, h] = jnp.where(mask, v, jnp.zeros_like(v))


def _mem_kv(mem, g, w, kg, bd):
    b, m, _ = mem.shape
    out = jax.ShapeDtypeStruct((b, MEM_HEADS, m, 256), BF16)
    spec = pl.BlockSpec((1, MEM_HEADS, m, 256), lambda bi: (bi, 0, 0, 0))
    return pl.pallas_call(
        _memkv_kernel,
        grid=(b,),
        in_specs=[pl.BlockSpec((1, m, D_MODEL), lambda bi: (bi, 0, 0)), _resident(g.shape), _resident(w.shape),
                  _resident(kg.shape), _resident(bd.shape)],
        out_specs=[spec, spec],
        out_shape=[out, out],
        compiler_params=_cparams(("parallel",)),
        name="mem_kv",
    )(mem, g, w, kg, bd)


def _memattn_kernel(x_ref, g_ref, wq_ref, qg_ref, bd_ref, k_ref, v_ref, wo_ref, o_ref):
    x = x_ref[0]
    xn = _rms(x, g_ref[...]).astype(BF16)
    q = _dot(xn, wq_ref[...])
    ss = _dot((q * q).astype(BF16), bd_ref[...])
    q = (q * lax.rsqrt(ss * (1.0 / MEM_DH) + EPS) * qg_ref[...] * (MEM_DH ** -0.5)).astype(BF16)
    o = jnp.zeros((x.shape[0], 256), F32)
    for h in range(MEM_HEADS):
        s = _dot_nt(q, k_ref[0, h])
        m = jnp.max(s, axis=-1, keepdims=True)
        p = jnp.exp(s - m)
        l = jnp.sum(p, axis=-1, keepdims=True)
        o = o + _dot((p / l).astype(BF16), v_ref[0, h])
    o_ref[0] = x + _dot(o.astype(BF16), wo_ref[...])


def _mem_attn(x3, g, wq, qg, bd, k, v, wo):
    b, s, _ = x3.shape
    m = k.shape[2]
    tok = pl.BlockSpec((1, MEM_TM, D_MODEL), lambda bi, i: (bi, i, 0))
    kv = pl.BlockSpec((1, MEM_HEADS, m, 256), lambda bi, i: (bi, 0, 0, 0))
    return pl.pallas_call(
        _memattn_kernel,
        grid=(b, s // MEM_TM),
        in_specs=[tok, _resident(g.shape), _resident(wq.shape), _resident(qg.shape), _resident(bd.shape), kv, kv,
                  _resident(wo.shape)],
        out_specs=tok,
        out_shape=jax.ShapeDtypeStruct((b, s, D_MODEL), F32),
        compiler_params=_cparams(("parallel", "parallel")),
        name="mem_attn",
    )(x3, g, wq, qg, bd, k, v, wo)


def _rope_tables(seq):
    half = MLA_ROPE // 2
    inv = 1.0 / (ROPE_THETA ** (jnp.arange(0, MLA_ROPE, 2, dtype=F32) / MLA_ROPE))
    ang = jnp.arange(seq, dtype=F32)[:, None] * inv[None, :]
    cos, sin = jnp.cos(ang), jnp.sin(ang)
    zeros = lambda w: jnp.zeros((seq, w), F32)
    pad = LANES - MLA_QK
    rc = jnp.concatenate([jnp.ones((seq, MLA_NOPE), F32), cos, cos, zeros(pad)], axis=1)
    rs1 = jnp.concatenate([zeros(MLA_NOPE), -sin, zeros(half), zeros(pad)], axis=1)
    rs2 = jnp.concatenate([zeros(MLA_NOPE), zeros(half), sin, zeros(pad)], axis=1)
    return rc, rs1, rs2


def _pad_lanes(a, width=LANES):
    return jnp.pad(a, [(0, 0)] * (a.ndim - 1) + [(0, width - a.shape[-1])])


def _layer_params(l, seq, a):
    row = lambda v: v[l].reshape(1, -1).astype(F32)
    p = {}
    for name in ("ffn1", "ffn2"):
        w_in = a[name + "_w_in"][l]
        p[name] = (row(a[name + "_norm"]), w_in[:, :D_FF].astype(BF16), w_in[:, D_FF:].astype(BF16),
                   a[name + "_w_out"][l].astype(BF16))
    w = a["w_mix_in"][l]
    o_ckv = MLA_Q_LORA
    o_kr = o_ckv + MLA_KV_LORA
    o_swa = o_kr + MLA_ROPE
    o_na = o_swa + (SWA_HEADS + 2 * SWA_KV_HEADS) * SWA_DH
    w_kr = jnp.pad(w[:, o_kr:o_swa], ((0, 0), (MLA_NOPE, LANES - MLA_QK)))
    p["w_mla"] = jnp.concatenate([w[:, :o_kr], w_kr], axis=1).astype(BF16)
    p["mix_norm"] = row(a["mix_norm"])
    p["mla_q_norm"] = row(a["mla_q_norm"])
    p["mla_kv_norm"] = row(a["mla_kv_norm"])
    w_uq = a["mla_w_uq"][l].reshape(MLA_Q_LORA, MLA_HEADS, MLA_QK)
    p["w_uq"] = _pad_lanes(w_uq).reshape(MLA_Q_LORA, MLA_HEADS * LANES).astype(BF16)
    w_ukv = a["mla_w_ukv"][l].reshape(MLA_KV_LORA, MLA_HEADS, MLA_NOPE + MLA_V)
    p["w_uk"] = _pad_lanes(w_ukv[:, :, :MLA_NOPE]).reshape(MLA_KV_LORA, MLA_HEADS * LANES).astype(BF16)
    p["w_uvt"] = w_ukv[:, :, MLA_NOPE:].reshape(MLA_KV_LORA, MLA_OUT).T.astype(BF16)
    p["mla_q_gain"] = _pad_lanes(row(a["mla_q_gain"]))
    p["mla_k_gain"] = _pad_lanes(row(a["mla_k_gain"]))
    p["rope_c"], p["rope_s1"], p["rope_s2"] = _rope_tables(seq)
    w_sq = w[:, o_swa:o_swa + SWA_OUT]
    w_sk = w[:, o_swa + SWA_OUT:o_swa + SWA_OUT + SWA_KV_HEADS * SWA_DH].reshape(D_MODEL, SWA_KV_HEADS, SWA_DH)
    w_sv = w[:, o_swa + SWA_OUT + SWA_KV_HEADS * SWA_DH:o_na].reshape(D_MODEL, SWA_KV_HEADS, SWA_DH)
    expand = lambda t: jnp.repeat(t, SWA_GROUP, axis=1).reshape(D_MODEL, SWA_OUT)
    p["w_swa"] = jnp.concatenate([w_sq, expand(w_sk), expand(w_sv)], axis=1).astype(BF16)
    p["swa_q_gain"] = jnp.tile(row(a["swa_q_gain"]), (1, SWA_HEADS))
    p["swa_k_gain"] = jnp.tile(row(a["swa_k_gain"]), (1, SWA_HEADS))
    p["swa_sink"] = a["swa_sink"][l].astype(F32)
    p["w_na"] = w[:, o_na:].astype(BF16)
    p["na_q_gain"] = jnp.tile(row(a["na_q_gain"]), (1, NA_HEADS))
    p["na_k_gain"] = jnp.tile(row(a["na_k_gain"]), (1, NA_HEADS))
    p["na_bias"] = _na_bias_table(a["na_rel_bias"][l].astype(F32), seq // GRID_W)
    g = row(a["grp_out_gain"])
    p["g_mla"], p["g_swa"], p["g_na"] = g[:, :MLA_OUT], g[:, MLA_OUT:MLA_OUT + SWA_OUT], g[:, MLA_OUT + SWA_OUT:]
    p["w_mix_out"] = a["w_mix_out"][l].astype(BF16)
    p["mem_norm_x"] = row(a["mem_norm_x"])
    p["mem_norm_m"] = row(a["mem_norm_m"])
    p["mem_w_q"] = a["mem_w_q"][l].astype(BF16)
    p["mem_w_kv"] = a["mem_w_kv"][l].astype(BF16)
    p["mem_q_gain"] = jnp.tile(row(a["mem_q_gain"]), (1, MEM_HEADS))
    p["mem_k_gain"] = jnp.tile(row(a["mem_k_gain"]), (1, MEM_HEADS))
    p["mem_w_o"] = a["mem_w_o"][l].astype(BF16)
    p["block_norm"] = row(a["block_norm"])
    seg = np.arange(256) // 64
    p["bd"] = jnp.asarray(seg[:, None] == seg[None, :], BF16)
    return p


def kernel(x, mem, ffn1_norm, ffn1_w_in, ffn1_w_out, mix_norm, w_mix_in, mla_q_norm, mla_w_uq, mla_kv_norm, mla_w_ukv, mla_q_gain, mla_k_gain, swa_q_gain, swa_k_gain, swa_sink, na_q_gain, na_k_gain, na_rel_bias, grp_out_gain, w_mix_out, mem_norm_x, mem_norm_m, mem_w_q, mem_w_kv, mem_q_gain, mem_k_gain, mem_w_o, ffn2_norm, ffn2_w_in, ffn2_w_out, block_norm):
    a = dict(ffn1_norm=ffn1_norm, ffn1_w_in=ffn1_w_in, ffn1_w_out=ffn1_w_out, mix_norm=mix_norm, w_mix_in=w_mix_in,
             mla_q_norm=mla_q_norm, mla_w_uq=mla_w_uq, mla_kv_norm=mla_kv_norm, mla_w_ukv=mla_w_ukv,
             mla_q_gain=mla_q_gain, mla_k_gain=mla_k_gain, swa_q_gain=swa_q_gain, swa_k_gain=swa_k_gain,
             swa_sink=swa_sink, na_q_gain=na_q_gain, na_k_gain=na_k_gain, na_rel_bias=na_rel_bias,
             grp_out_gain=grp_out_gain, w_mix_out=w_mix_out, mem_norm_x=mem_norm_x, mem_norm_m=mem_norm_m,
             mem_w_q=mem_w_q, mem_w_kv=mem_w_kv, mem_q_gain=mem_q_gain, mem_k_gain=mem_k_gain, mem_w_o=mem_w_o,
             ffn2_norm=ffn2_norm, ffn2_w_in=ffn2_w_in, ffn2_w_out=ffn2_w_out, block_norm=block_norm)
    b, s, d = x.shape
    depth = ffn1_norm.shape[0]
    swa_bias = _swa_bias_table(s)
    x = x.astype(F32)
    mem = mem.astype(F32)
    for l in range(depth):
        p = _layer_params(l, s, a)
        x2 = _ffn(x.reshape(b * s, d), *p["ffn1"])
        q, k, vt, sq, sk, sv, nq, nk, nv = _prep(x2.reshape(b, s, d), p)
        o_mla = _mla_attn(q, k, vt)
        o_swa = _swa_attn(sq, sk, sv, swa_bias, p["swa_sink"], p["g_swa"])
        o_na = _na_attn(nq, nk, nv, p["na_bias"], p["g_na"])
        x2 = _out_proj(x2, o_mla.reshape(b * s, MLA_OUT), o_swa.reshape(b * s, SWA_OUT),
                       o_na.reshape(b * s, NA_OUT), p["g_mla"], p["w_mix_out"])
        mk, mv = _mem_kv(mem, p["mem_norm_m"], p["mem_w_kv"], p["mem_k_gain"], p["bd"])
        x3 = _mem_attn(x2.reshape(b, s, d), p["mem_norm_x"], p["mem_w_q"], p["mem_q_gain"], p["bd"], mk, mv,
                       p["mem_w_o"])
        x = _ffn(x3.reshape(b * s, d), *p["ffn2"], final_g=p["block_norm"])
    return x.reshape(b, s, d)
```

```python
import functools

import numpy as np
import jax
import jax.numpy as jnp
from jax import lax
from jax.experimental import pallas as pl
from jax.experimental.pallas import tpu as pltpu

F32 = jnp.float32
BF16 = jnp.bfloat16

D_MODEL = 1024
D_FF = 2816
N_MEM = 256
GRID_W = 64
EPS = 1e-6
NEG_INF = -1e30
LOG2_E = 1.4426950408889634

MLA_HEADS = 8
MLA_Q_LORA = 256
MLA_KV_LORA = 128
MLA_NOPE = 64
MLA_ROPE = 32
MLA_QK = MLA_NOPE + MLA_ROPE
MLA_V = 64
ROPE_THETA = 10000.0
MLA_OUT = MLA_HEADS * MLA_V

SWA_HEADS = 4
SWA_KV_HEADS = 2
SWA_GROUP = SWA_HEADS // SWA_KV_HEADS
SWA_DH = 64
SWA_WINDOW = 128
SWA_OUT = SWA_HEADS * SWA_DH

NA_HEADS = 4
NA_DH = 64
NA_KR = 8
NA_KC = 16
NA_OUT = NA_HEADS * NA_DH

MEM_HEADS = 4
MEM_DH = 64

LANES = 128
V7X_VMEM_BYTES = 64 * 2**20
VMEM_LIMIT = 56 * 2**20

FFN_TM = 512
PREP_TM = 512
MLA_TQ = 512
MLA_TK = 512
MLA_NQ = 4
MLA_HG = 4
MLA_ACC_ROWS = MLA_V + 16
SWA_TQ = 256
SWA_WIN = SWA_TQ + 2 * SWA_WINDOW
NA_TQ = 4 * GRID_W
NA_WROWS = 12
NA_WIN = NA_WROWS * GRID_W
TAIL_TM = 512


def _cparams(sem):
    return pltpu.CompilerParams(dimension_semantics=sem, vmem_limit_bytes=VMEM_LIMIT)


def _rms(x, g):
    ms = jnp.mean(x * x, axis=-1, keepdims=True)
    return x * lax.rsqrt(ms + EPS) * g


def _dot(a, b):
    return jnp.dot(a, b, preferred_element_type=F32)


def _dot_nt(a, b):
    return lax.dot_general(a, b, (((1,), (1,)), ((), ())), preferred_element_type=F32)


def _resident(shape):
    nd = len(shape)
    return pl.BlockSpec(shape, lambda *_: (0,) * nd)


def _ffn_kernel(x_ref, g_ref, wg_ref, wu_ref, wo_ref, o_ref):
    x = x_ref[...]
    xn = _rms(x, g_ref[...]).astype(BF16)
    g = _dot(xn, wg_ref[...])
    u = _dot(xn, wu_ref[...])
    a = (g * jax.nn.sigmoid(g) * u).astype(BF16)
    o_ref[...] = x + 0.5 * _dot(a, wo_ref[...])


def _ffn(x2, g, wg, wu, wo):
    n = x2.shape[0]
    single = pl.Buffered(1)
    return pl.pallas_call(
        _ffn_kernel,
        grid=(n // FFN_TM,),
        in_specs=[
            pl.BlockSpec((FFN_TM, D_MODEL), lambda i: (i, 0)),
            _resident((1, D_MODEL)),
            pl.BlockSpec((D_MODEL, D_FF), lambda i: (0, 0), pipeline_mode=single),
            pl.BlockSpec((D_MODEL, D_FF), lambda i: (0, 0), pipeline_mode=single),
            pl.BlockSpec((D_FF, D_MODEL), lambda i: (0, 0), pipeline_mode=single),
        ],
        out_specs=pl.BlockSpec((FFN_TM, D_MODEL), lambda i: (i, 0)),
        out_shape=jax.ShapeDtypeStruct((n, D_MODEL), F32),
        compiler_params=_cparams(("parallel",)),
        name="ffn",
    )(x2, g, wg, wu, wo)


def _prep_kernel(x_ref, mixg_ref, wmla_ref, qng_ref, kvng_ref, wuq_ref, wuk_ref, wuvt_ref,
                 qgain_ref, kgain_ref, rc_ref, rs_ref, rct_ref, rst_ref,
                 wswa_ref, sqg_ref, skg_ref, wna_ref, nqg_ref, nkg_ref, bd_ref,
                 q_ref, k_ref, vt_ref, sq_ref, sk_ref, sv_ref, nq_ref, nk_ref, nv_ref):
    xn = _rms(x_ref[0], mixg_ref[...]).astype(BF16)

    z = _dot(xn, wmla_ref[...])
    cqn = _rms(z[:, :MLA_Q_LORA], qng_ref[...]).astype(BF16)
    ckvn = _rms(z[:, MLA_Q_LORA:MLA_Q_LORA + MLA_KV_LORA], kvng_ref[...]).astype(BF16)
    kr = z[:, MLA_Q_LORA + MLA_KV_LORA:]
    qat = _dot_nt(wuq_ref[...], cqn)
    ka = _dot(ckvn, wuk_ref[...])
    vt_ref[0, 0] = _dot_nt(wuvt_ref[...], ckvn).astype(BF16)

    rc, rs, kgain = rc_ref[...], rs_ref[...], kgain_ref[...]
    for h in range(MLA_HEADS):
        t = ka[:, h * LANES:(h + 1) * LANES] + kr
        ss = jnp.sum(t * t, axis=-1, keepdims=True)
        tn = t * lax.rsqrt(ss * (1.0 / MLA_QK) + EPS) * kgain
        k_ref[0, h] = (tn * rc + pltpu.roll(tn, LANES // 2, 1) * rs).astype(BF16)

    rct, rst, qgain = rct_ref[...], rst_ref[...], qgain_ref[...]
    for h in range(MLA_HEADS):
        t = qat[h * LANES:(h + 1) * LANES, :]
        ss = jnp.sum(t * t, axis=0, keepdims=True)
        tn = t * lax.rsqrt(ss * (1.0 / MLA_QK) + EPS) * qgain
        swapped = jnp.concatenate([tn[LANES // 2:], tn[:LANES // 2]], axis=0)
        q_ref[0, 0, h] = ((tn * rct + swapped * rst) * (MLA_QK ** -0.5 * LOG2_E)).astype(BF16)

    bd = bd_ref[...]

    def head_norm(t, gain):
        ss = _dot((t * t).astype(BF16), bd)
        return t * lax.rsqrt(ss * (1.0 / SWA_DH) + EPS) * gain

    zs = _dot(xn, wswa_ref[...])
    sq_ref[0] = (head_norm(zs[:, :256], sqg_ref[...]) * (SWA_DH ** -0.5)).astype(BF16)
    sk_ref[0] = head_norm(zs[:, 256:512], skg_ref[...]).astype(BF16)
    sv_ref[0] = zs[:, 512:].astype(BF16)

    zn = _dot(xn, wna_ref[...])
    nq_ref[0] = (head_norm(zn[:, :256], nqg_ref[...]) * (NA_DH ** -0.5)).astype(BF16)
    nk_ref[0] = head_norm(zn[:, 256:512], nkg_ref[...]).astype(BF16)
    nv_ref[0] = zn[:, 512:].astype(BF16)


def _prep(x3, p):
    b, s, _ = x3.shape
    nt = s // PREP_TM
    tok = lambda w: pl.BlockSpec((1, PREP_TM, w), lambda bi, i: (bi, i, 0))
    rope = pl.BlockSpec((PREP_TM, LANES), lambda bi, i: (i, 0))
    rope_t = pl.BlockSpec((LANES, PREP_TM), lambda bi, i: (0, i))
    weights = [p["mix_norm"], p["w_mla"], p["mla_q_norm"], p["mla_kv_norm"], p["w_uq_t"], p["w_uk"], p["w_uvt"],
               p["mla_q_gain_col"], p["mla_k_gain"]]
    tail = [p["w_swa"], p["swa_q_gain"], p["swa_k_gain"], p["w_na"], p["na_q_gain"], p["na_k_gain"], p["bd"]]
    in_specs = ([tok(D_MODEL)] + [_resident(w.shape) for w in weights] + [rope, rope, rope_t, rope_t]
                + [_resident(w.shape) for w in tail])
    out_specs = [pl.BlockSpec((1, 1, MLA_HEADS, LANES, PREP_TM), lambda bi, i: (bi, i, 0, 0, 0)),
                 pl.BlockSpec((1, MLA_HEADS, PREP_TM, LANES), lambda bi, i: (bi, 0, i, 0)),
                 pl.BlockSpec((1, 1, MLA_OUT, PREP_TM), lambda bi, i: (bi, i, 0, 0))] + [tok(256)] * 6
    tok_shape = jax.ShapeDtypeStruct((b, s, 256), BF16)
    out_shape = [jax.ShapeDtypeStruct((b, nt, MLA_HEADS, LANES, PREP_TM), BF16),
                 jax.ShapeDtypeStruct((b, MLA_HEADS, s, LANES), BF16),
                 jax.ShapeDtypeStruct((b, nt, MLA_OUT, PREP_TM), BF16)] + [tok_shape] * 6
    return pl.pallas_call(
        _prep_kernel,
        grid=(b, nt),
        in_specs=in_specs,
        out_specs=out_specs,
        out_shape=out_shape,
        compiler_params=_cparams(("parallel", "parallel")),
        name="mix_prep",
    )(x3, *weights, p["rope_c"], p["rope_s"], p["rope_ct"], p["rope_st"], *tail)


def _mla_kernel(q_ref, k_ref, vt_ref, o_ref, s_scr, p_scr, bmax_scr, alpha_scr, acc_scr, m_scr, *, nkb):
    heads = range(MLA_HG)
    n_items = MLA_NQ * nkb
    ones_rows = jnp.ones((MLA_ACC_ROWS - MLA_V, MLA_TK), BF16)

    def split(n):
        if isinstance(n, int):
            return n // nkb, n % nkb
        return lax.div(n, nkb), lax.rem(n, nkb)

    def scores(n, slot, hs=heads):
        qb, kb = split(n)
        row0 = kb * MLA_TK if isinstance(kb, int) else pl.multiple_of(kb * MLA_TK, MLA_TK)
        for hh in hs:
            s = _dot(k_ref[0, hh, pl.ds(row0, MLA_TK), :], q_ref[0, qb, hh])
            s_scr[slot, hh] = s
            bmax_scr[slot, hh] = jnp.max(s, axis=0, keepdims=True)

    def softmax(n, slot, hs=heads):
        first_key_block = split(n)[1] == 0
        for hh in hs:
            m_old = jnp.where(first_key_block, NEG_INF, m_scr[hh])
            m_new = jnp.maximum(m_old, bmax_scr[slot, hh])
            m_scr[hh] = m_new
            alpha_scr[slot, hh] = jnp.exp2(m_old - m_new)
            p_scr[slot, hh] = jnp.exp2(s_scr[slot, hh] - m_new).astype(BF16)

    def accumulate(n, slot, hs=heads):
        kb = split(n)[1]
        for hh in hs:
            vt1 = jnp.concatenate([vt_ref[0, kb, hh * MLA_V:(hh + 1) * MLA_V, :], ones_rows], axis=0)
            acc_scr[hh] = alpha_scr[slot, hh] * acc_scr[hh] + _dot(vt1, p_scr[slot, hh])

    def flush(n):
        qb = split(n)[0]
        q0 = qb * MLA_TQ if isinstance(qb, int) else pl.multiple_of(qb * MLA_TQ, MLA_TQ)
        for hp in range(MLA_HG // 2):
            pair_t = jnp.concatenate(
                [acc_scr[hh, :MLA_V] / acc_scr[hh, MLA_V:MLA_V + 1] for hh in (2 * hp, 2 * hp + 1)], axis=0)
            o_ref[0, pl.ds(q0, MLA_TQ), hp * LANES:(hp + 1) * LANES] = pair_t.T

    def step(n, slot):
        for hh in heads:
            scores(n, slot, (hh,))
            softmax(n - 1, 1 - slot, (hh,))
            accumulate(n - 1, 1 - slot, (hh,))
        pl.when(split(n)[1] == 0)(functools.partial(flush, n - 1))

    acc_scr[...] = jnp.zeros_like(acc_scr)
    m_scr[...] = jnp.full_like(m_scr, NEG_INF)
    scores(0, 0)

    def trip(n, carry):
        for slot in range(2):
            pl.when(jnp.bitwise_and(n, 1) == slot)(functools.partial(step, n, slot))
        return carry

    lax.fori_loop(1, n_items, trip, 0)
    for hh in heads:
        softmax(n_items - 1, 1, (hh,))
        accumulate(n_items - 1, 1, (hh,))
    flush(n_items - 1)


def _mla_attn(q, k, vt):
    b, h, s, _ = k.shape
    nkb = s // MLA_TK
    tq_step = MLA_NQ * MLA_TQ
    assert (MLA_NQ * nkb) % 2 == 0 and h % MLA_HG == 0 and s % tq_step == 0
    assert MLA_TK == PREP_TM and MLA_TQ == PREP_TM
    return pl.pallas_call(
        functools.partial(_mla_kernel, nkb=nkb),
        grid=(b, h // MLA_HG, s // tq_step),
        in_specs=[
            pl.BlockSpec((1, MLA_NQ, MLA_HG, LANES, MLA_TQ), lambda bi, hg, i: (bi, i, hg, 0, 0)),
            pl.BlockSpec((1, MLA_HG, s, LANES), lambda bi, hg, i: (bi, hg, 0, 0)),
            pl.BlockSpec((1, nkb, MLA_HG * MLA_V, MLA_TK), lambda bi, hg, i: (bi, 0, hg, 0)),
        ],
        out_specs=pl.BlockSpec((1, tq_step, MLA_HG * MLA_V), lambda bi, hg, i: (bi, i, hg)),
        out_shape=jax.ShapeDtypeStruct((b, s, MLA_OUT), F32),
        scratch_shapes=[pltpu.VMEM((2, MLA_HG, MLA_TK, MLA_TQ), F32),
                        pltpu.VMEM((2, MLA_HG, MLA_TK, MLA_TQ), BF16),
                        pltpu.VMEM((2, MLA_HG, 1, MLA_TQ), F32),
                        pltpu.VMEM((2, MLA_HG, 1, MLA_TQ), F32),
                        pltpu.VMEM((MLA_HG, MLA_ACC_ROWS, MLA_TQ), F32),
                        pltpu.VMEM((MLA_HG, 1, MLA_TQ), F32)],
        compiler_params=_cparams(("parallel", "parallel", "arbitrary")),
        name="mla_attn",
    )(q, k, vt)


def _lane_head_mask(h):
    lane = lax.broadcasted_iota(jnp.int32, (1, 256), 1)
    return (lane >= h * 64) & (lane < (h + 1) * 64)


def _window_attn(q, kw, vw, bias_ref, sinks):
    o = jnp.zeros((q.shape[0], 256), F32)
    for h in range(4):
        mask = _lane_head_mask(h)
        s = _dot_nt(jnp.where(mask, q, jnp.zeros_like(q)), kw) + bias_ref[0, h]
        m = jnp.max(s, axis=-1, keepdims=True)
        if sinks is not None:
            m = jnp.maximum(m, sinks[h])
        p = jnp.exp(s - m)
        l = jnp.sum(p, axis=-1, keepdims=True)
        if sinks is not None:
            l = l + jnp.exp(sinks[h] - m)
        o = o + jnp.where(mask, _dot(p.astype(BF16), vw) * (1.0 / l), 0.0)
    return o


def _swa_kernel(sink_ref, q_ref, k_ref, v_ref, bias_ref, g_ref, o_ref, *, seq):
    i = pl.program_id(1)
    w0 = pl.multiple_of(jnp.clip(i * SWA_TQ - SWA_WINDOW, 0, seq - SWA_WIN), SWA_WINDOW)
    kw = k_ref[0, pl.ds(w0, SWA_WIN), :]
    vw = v_ref[0, pl.ds(w0, SWA_WIN), :]
    sinks = [sink_ref[h] for h in range(SWA_HEADS)]
    o = _window_attn(q_ref[0], kw, vw, bias_ref, sinks)
    o_ref[0] = _rms(o, g_ref[...]).astype(BF16)


def _na_kernel(q_ref, k_ref, v_ref, bias_ref, g_ref, o_ref, *, rows):
    i = pl.program_id(1)
    r0 = jnp.clip(i * (NA_TQ // GRID_W) - NA_KR // 2, 0, rows - NA_WROWS)
    w0 = pl.multiple_of(r0 * GRID_W, GRID_W)
    kw = k_ref[0, pl.ds(w0, NA_WIN), :]
    vw = v_ref[0, pl.ds(w0, NA_WIN), :]
    o = _window_attn(q_ref[0], kw, vw, bias_ref, None)
    o_ref[0] = _rms(o, g_ref[...]).astype(BF16)


def _edge_pattern(i, n):
    return (i > 0).astype(jnp.int32) + (i == n - 1).astype(jnp.int32)


def _swa_attn(q, k, v, bias, sink, g):
    b, s, _ = q.shape
    nq = s // SWA_TQ
    return pl.pallas_call(
        functools.partial(_swa_kernel, seq=s),
        grid=(b, nq),
        in_specs=[
            pl.BlockSpec(memory_space=pltpu.SMEM),
            pl.BlockSpec((1, SWA_TQ, 256), lambda bi, i: (bi, i, 0)),
            pl.BlockSpec((1, s, 256), lambda bi, i: (bi, 0, 0)),
            pl.BlockSpec((1, s, 256), lambda bi, i: (bi, 0, 0)),
            pl.BlockSpec((1, SWA_HEADS, SWA_TQ, SWA_WIN), lambda bi, i: (_edge_pattern(i, nq), 0, 0, 0)),
            _resident((1, SWA_OUT)),
        ],
        out_specs=pl.BlockSpec((1, SWA_TQ, 256), lambda bi, i: (bi, i, 0)),
        out_shape=jax.ShapeDtypeStruct((b, s, SWA_OUT), BF16),
        compiler_params=_cparams(("parallel", "arbitrary")),
        name="swa_attn",
    )(sink, q, k, v, bias, g)


def _na_attn(q, k, v, bias, g):
    b, s, _ = q.shape
    nq = s // NA_TQ
    return pl.pallas_call(
        functools.partial(_na_kernel, rows=s // GRID_W),
        grid=(b, nq),
        in_specs=[
            pl.BlockSpec((1, NA_TQ, 256), lambda bi, i: (bi, i, 0)),
            pl.BlockSpec((1, s, 256), lambda bi, i: (bi, 0, 0)),
            pl.BlockSpec((1, s, 256), lambda bi, i: (bi, 0, 0)),
            pl.BlockSpec((1, NA_HEADS, NA_TQ, NA_WIN), lambda bi, i: (_edge_pattern(i, nq), 0, 0, 0)),
            _resident((1, NA_OUT)),
        ],
        out_specs=pl.BlockSpec((1, NA_TQ, 256), lambda bi, i: (bi, i, 0)),
        out_shape=jax.ShapeDtypeStruct((b, s, NA_OUT), BF16),
        compiler_params=_cparams(("parallel", "arbitrary")),
        name="na_attn",
    )(q, k, v, bias, g)


def _swa_bias_table(seq):
    slopes = 2.0 ** (-8.0 * np.arange(1, SWA_HEADS + 1, dtype=np.float64) / SWA_HEADS)
    ql = np.arange(SWA_TQ)[:, None]
    kl = np.arange(SWA_WIN)[None, :]
    pats = []
    for q0, w0 in ((0, 0), (SWA_TQ, SWA_TQ - SWA_WINDOW), (seq - SWA_TQ, seq - SWA_WIN)):
        dist = np.abs(ql + q0 - (kl + w0))
        pats.append(np.where(dist[None] <= SWA_WINDOW, -slopes[:, None, None] * dist[None], NEG_INF))
    return jnp.asarray(np.stack(pats), F32)


def _na_bias_table(rel_bias, rows):
    qr_l = np.arange(NA_TQ // GRID_W)
    kr_l = np.arange(NA_WROWS)
    qc = np.arange(GRID_W)[:, None]
    kc = np.arange(GRID_W)[None, :]
    c0 = np.clip(qc - NA_KC // 2, 0, GRID_W - NA_KC)
    col_ok = (kc >= c0) & (kc < c0 + NA_KC)
    dc = np.clip(kc - qc + (NA_KC - 1), 0, 2 * NA_KC - 2)
    pick_dc = (dc[None] == np.arange(2 * NA_KC - 1)[:, None, None]).astype(np.float32)
    last_first = rows - NA_TQ // GRID_W
    pick_dr, ok = [], []
    for r_first, w_row0 in ((0, 0), (NA_KR // 2, 0), (last_first, rows - NA_WROWS)):
        rq = r_first + qr_l[:, None]
        rk = w_row0 + kr_l[None, :]
        r0 = np.clip(rq - NA_KR // 2, 0, rows - NA_KR)
        row_ok = (rk >= r0) & (rk < r0 + NA_KR)
        dr = np.clip(rk - rq + (NA_KR - 1), 0, 2 * NA_KR - 2)
        pick_dr.append((dr[None] == np.arange(2 * NA_KR - 1)[:, None, None]).astype(np.float32))
        ok.append(row_ok[:, None, :, None] & col_ok[None, :, None, :])
    hi = lax.Precision.HIGHEST
    t = jnp.einsum("hdc,pdqk->phqkc", rel_bias, jnp.asarray(np.stack(pick_dr)), precision=hi)
    t = jnp.einsum("phqkc,cxy->phqxky", t, jnp.asarray(pick_dc), precision=hi)
    t = jnp.where(jnp.asarray(np.stack(ok))[:, None], t, NEG_INF)
    return t.reshape(3, NA_HEADS, NA_TQ, NA_WIN).astype(F32)


def _memkv_kernel(mem_ref, g_ref, w_ref, kg_ref, bd_ref, k_ref, v_ref):
    m = _rms(mem_ref[0], g_ref[...]).astype(BF16)
    kv = _dot(m, w_ref[...])
    k = kv[:, :256]
    ss = _dot((k * k).astype(BF16), bd_ref[...])
    k = (k * lax.rsqrt(ss * (1.0 / MEM_DH) + EPS) * kg_ref[...]).astype(BF16)
    v = kv[:, 256:].astype(BF16)
    for h in range(MEM_HEADS):
        mask = _lane_head_mask(h)
        k_ref[0, h] = jnp.where(mask, k, jnp.zeros_like(k))
        v_ref[0, h] = jnp.where(mask, v, jnp.zeros_like(v))


def _mem_kv(mem, g, w, kg, bd):
    b, m, _ = mem.shape
    out = jax.ShapeDtypeStruct((b, MEM_HEADS, m, 256), BF16)
    spec = pl.BlockSpec((1, MEM_HEADS, m, 256), lambda bi: (bi, 0, 0, 0))
    return pl.pallas_call(
        _memkv_kernel,
        grid=(b,),
        in_specs=[pl.BlockSpec((1, m, D_MODEL), lambda bi: (bi, 0, 0)), _resident(g.shape), _resident(w.shape),
                  _resident(kg.shape), _resident(bd.shape)],
        out_specs=[spec, spec],
        out_shape=[out, out],
        compiler_params=_cparams(("parallel",)),
        name="mem_kv",
    )(mem, g, w, kg, bd)


def _tail_kernel(x_ref, om_ref, os_ref, on_ref, gm_ref, wmix_ref,
                 mg_ref, wq_ref, qg_ref, bd_ref, k_ref, v_ref, wmo_ref,
                 fg_ref, wg_ref, wu_ref, wo_ref, bg_ref, o_ref):
    om = _rms(om_ref[0], gm_ref[...]).astype(BF16)
    cat = jnp.concatenate([om, os_ref[0], on_ref[0]], axis=-1)
    x = x_ref[0] + _dot(cat, wmix_ref[...])

    xn = _rms(x, mg_ref[...]).astype(BF16)
    q = _dot(xn, wq_ref[...])
    ss = _dot((q * q).astype(BF16), bd_ref[...])
    q = (q * lax.rsqrt(ss * (1.0 / MEM_DH) + EPS) * qg_ref[...] * (MEM_DH ** -0.5)).astype(BF16)
    o = jnp.zeros((x.shape[0], 256), F32)
    for h in range(MEM_HEADS):
        s = _dot_nt(q, k_ref[0, h])
        m = jnp.max(s, axis=-1, keepdims=True)
        p = jnp.exp(s - m)
        l = jnp.sum(p, axis=-1, keepdims=True)
        o = o + _dot(p.astype(BF16), v_ref[0, h]) * (1.0 / l)
    x = x + _dot(o.astype(BF16), wmo_ref[...])

    xn = _rms(x, fg_ref[...]).astype(BF16)
    g = _dot(xn, wg_ref[...])
    u = _dot(xn, wu_ref[...])
    a = (g * jax.nn.sigmoid(g) * u).astype(BF16)
    x = x + 0.5 * _dot(a, wo_ref[...])
    o_ref[0] = _rms(x, bg_ref[...])


def _tail(x3, o_mla, o_swa, o_na, mk, mv, p):
    b, s, _ = x3.shape
    m = mk.shape[2]
    tok = lambda w: pl.BlockSpec((1, TAIL_TM, w), lambda bi, i: (bi, i, 0))
    kv = pl.BlockSpec((1, MEM_HEADS, m, 256), lambda bi, i: (bi, 0, 0, 0))
    single = lambda a: pl.BlockSpec(a.shape, lambda bi, i: (0,) * a.ndim, pipeline_mode=pl.Buffered(1))
    fg, wg, wu, wo = p["ffn2"]
    mix = [p["g_mla"], p["w_mix_out"]]
    mem = [p["mem_norm_x"], p["mem_w_q"], p["mem_q_gain"], p["bd"]]
    ffn = [fg, wg, wu, wo, p["block_norm"]]
    in_specs = ([tok(D_MODEL), tok(MLA_OUT), tok(SWA_OUT), tok(NA_OUT)] + [single(a) for a in mix]
                + [single(a) for a in mem] + [kv, kv, single(p["mem_w_o"])] + [single(a) for a in ffn])
    return pl.pallas_call(
        _tail_kernel,
        grid=(b, s // TAIL_TM),
        in_specs=in_specs,
        out_specs=tok(D_MODEL),
        out_shape=jax.ShapeDtypeStruct((b, s, D_MODEL), F32),
        compiler_params=_cparams(("parallel", "parallel")),
        name="layer_tail",
    )(x3, o_mla, o_swa, o_na, *mix, *mem, mk, mv, p["mem_w_o"], *ffn)


def _head_lanes(a):
    half = MLA_ROPE // 2
    x1 = a[..., MLA_NOPE:MLA_NOPE + half]
    x2 = a[..., MLA_NOPE + half:MLA_QK]
    split = LANES // 2 - half
    pad = jnp.zeros(a.shape[:-1] + (LANES - MLA_QK,), a.dtype)
    return jnp.concatenate([x2, a[..., :split], x1, a[..., split:MLA_NOPE], pad], axis=-1)


def _rope_tables(seq):
    inv = 1.0 / (ROPE_THETA ** (jnp.arange(0, MLA_ROPE, 2, dtype=F32) / MLA_ROPE))
    ang = jnp.arange(seq, dtype=F32)[:, None] * inv[None, :]
    cos, sin = jnp.cos(ang), jnp.sin(ang)
    rc = _head_lanes(jnp.concatenate([jnp.ones((seq, MLA_NOPE), F32), cos, cos], axis=1))
    rs = _head_lanes(jnp.concatenate([jnp.zeros((seq, MLA_NOPE), F32), -sin, sin], axis=1))
    return rc, rs


def _layer_params(l, seq, a):
    row = lambda v: v[l].reshape(1, -1).astype(F32)
    p = {}
    for name in ("ffn1", "ffn2"):
        w_in = a[name + "_w_in"][l]
        p[name] = (row(a[name + "_norm"]), w_in[:, :D_FF].astype(BF16), w_in[:, D_FF:].astype(BF16),
                   a[name + "_w_out"][l].astype(BF16))
    w = a["w_mix_in"][l]
    o_ckv = MLA_Q_LORA
    o_kr = o_ckv + MLA_KV_LORA
    o_swa = o_kr + MLA_ROPE
    o_na = o_swa + (SWA_HEADS + 2 * SWA_KV_HEADS) * SWA_DH
    w_kr = _head_lanes(jnp.pad(w[:, o_kr:o_swa], ((0, 0), (MLA_NOPE, 0))))
    p["w_mla"] = jnp.concatenate([w[:, :o_kr], w_kr], axis=1).astype(BF16)
    p["mix_norm"] = row(a["mix_norm"])
    p["mla_q_norm"] = row(a["mla_q_norm"])
    p["mla_kv_norm"] = row(a["mla_kv_norm"])
    w_uq = a["mla_w_uq"][l].reshape(MLA_Q_LORA, MLA_HEADS, MLA_QK)
    p["w_uq_t"] = _head_lanes(w_uq).reshape(MLA_Q_LORA, MLA_HEADS * LANES).T.astype(BF16)
    w_ukv = a["mla_w_ukv"][l].reshape(MLA_KV_LORA, MLA_HEADS, MLA_NOPE + MLA_V)
    w_uk = jnp.pad(w_ukv[:, :, :MLA_NOPE], ((0, 0), (0, 0), (0, MLA_ROPE)))
    p["w_uk"] = _head_lanes(w_uk).reshape(MLA_KV_LORA, MLA_HEADS * LANES).astype(BF16)
    p["w_uvt"] = w_ukv[:, :, MLA_NOPE:].reshape(MLA_KV_LORA, MLA_OUT).T.astype(BF16)
    p["mla_q_gain_col"] = _head_lanes(row(a["mla_q_gain"])).reshape(LANES, 1)
    p["mla_k_gain"] = _head_lanes(row(a["mla_k_gain"]))
    p["rope_c"], p["rope_s"] = _rope_tables(seq)
    p["rope_ct"], p["rope_st"] = p["rope_c"].T, p["rope_s"].T
    w_sq = w[:, o_swa:o_swa + SWA_OUT]
    w_sk = w[:, o_swa + SWA_OUT:o_swa + SWA_OUT + SWA_KV_HEADS * SWA_DH].reshape(D_MODEL, SWA_KV_HEADS, SWA_DH)
    w_sv = w[:, o_swa + SWA_OUT + SWA_KV_HEADS * SWA_DH:o_na].reshape(D_MODEL, SWA_KV_HEADS, SWA_DH)
    expand = lambda t: jnp.repeat(t, SWA_GROUP, axis=1).reshape(D_MODEL, SWA_OUT)
    p["w_swa"] = jnp.concatenate([w_sq, expand(w_sk), expand(w_sv)], axis=1).astype(BF16)
    p["swa_q_gain"] = jnp.tile(row(a["swa_q_gain"]), (1, SWA_HEADS))
    p["swa_k_gain"] = jnp.tile(row(a["swa_k_gain"]), (1, SWA_HEADS))
    p["swa_sink"] = a["swa_sink"][l].astype(F32)
    p["w_na"] = w[:, o_na:].astype(BF16)
    p["na_q_gain"] = jnp.tile(row(a["na_q_gain"]), (1, NA_HEADS))
    p["na_k_gain"] = jnp.tile(row(a["na_k_gain"]), (1, NA_HEADS))
    p["na_bias"] = _na_bias_table(a["na_rel_bias"][l].astype(F32), seq // GRID_W)
    g = row(a["grp_out_gain"])
    p["g_mla"], p["g_swa"], p["g_na"] = g[:, :MLA_OUT], g[:, MLA_OUT:MLA_OUT + SWA_OUT], g[:, MLA_OUT + SWA_OUT:]
    p["w_mix_out"] = a["w_mix_out"][l].astype(BF16)
    p["mem_norm_x"] = row(a["mem_norm_x"])
    p["mem_norm_m"] = row(a["mem_norm_m"])
    p["mem_w_q"] = a["mem_w_q"][l].astype(BF16)
    p["mem_w_kv"] = a["mem_w_kv"][l].astype(BF16)
    p["mem_q_gain"] = jnp.tile(row(a["mem_q_gain"]), (1, MEM_HEADS))
    p["mem_k_gain"] = jnp.tile(row(a["mem_k_gain"]), (1, MEM_HEADS))
    p["mem_w_o"] = a["mem_w_o"][l].astype(BF16)
    p["block_norm"] = row(a["block_norm"])
    seg = np.arange(256) // 64
    p["bd"] = jnp.asarray(seg[:, None] == seg[None, :], BF16)
    return p


def kernel(x, mem, ffn1_norm, ffn1_w_in, ffn1_w_out, mix_norm, w_mix_in, mla_q_norm, mla_w_uq, mla_kv_norm, mla_w_ukv, mla_q_gain, mla_k_gain, swa_q_gain, swa_k_gain, swa_sink, na_q_gain, na_k_gain, na_rel_bias, grp_out_gain, w_mix_out, mem_norm_x, mem_norm_m, mem_w_q, mem_w_kv, mem_q_gain, mem_k_gain, mem_w_o, ffn2_norm, ffn2_w_in, ffn2_w_out, block_norm):
    a = dict(ffn1_norm=ffn1_norm, ffn1_w_in=ffn1_w_in, ffn1_w_out=ffn1_w_out, mix_norm=mix_norm, w_mix_in=w_mix_in,
             mla_q_norm=mla_q_norm, mla_w_uq=mla_w_uq, mla_kv_norm=mla_kv_norm, mla_w_ukv=mla_w_ukv,
             mla_q_gain=mla_q_gain, mla_k_gain=mla_k_gain, swa_q_gain=swa_q_gain, swa_k_gain=swa_k_gain,
             swa_sink=swa_sink, na_q_gain=na_q_gain, na_k_gain=na_k_gain, na_rel_bias=na_rel_bias,
             grp_out_gain=grp_out_gain, w_mix_out=w_mix_out, mem_norm_x=mem_norm_x, mem_norm_m=mem_norm_m,
             mem_w_q=mem_w_q, mem_w_kv=mem_w_kv, mem_q_gain=mem_q_gain, mem_k_gain=mem_k_gain, mem_w_o=mem_w_o,
             ffn2_norm=ffn2_norm, ffn2_w_in=ffn2_w_in, ffn2_w_out=ffn2_w_out, block_norm=block_norm)
    b, s, d = x.shape
    depth = ffn1_norm.shape[0]
    swa_bias = _swa_bias_table(s)
    x = x.astype(F32)
    mem = mem.astype(F32)
    for l in range(depth):
        p = _layer_params(l, s, a)
        x2 = _ffn(x.reshape(b * s, d), *p["ffn1"])
        q, k, vt, sq, sk, sv, nq, nk, nv = _prep(x2.reshape(b, s, d), p)
        o_mla = _mla_attn(q, k, vt)
        o_swa = _swa_attn(sq, sk, sv, swa_bias, p["swa_sink"], p["g_swa"])
        o_na = _na_attn(nq, nk, nv, p["na_bias"], p["g_na"])
        mk, mv = _mem_kv(mem, p["mem_norm_m"], p["mem_w_kv"], p["mem_k_gain"], p["bd"])
        x = _tail(x2.reshape(b, s, d), o_mla, o_swa, o_na, mk, mv, p)
    return x
```

```python
import functools

import numpy as np
import jax
import jax.numpy as jnp
from jax import lax
from jax.experimental import pallas as pl
from jax.experimental.pallas import tpu as pltpu

F32 = jnp.float32
BF16 = jnp.bfloat16

D_MODEL = 1024
D_FF = 2816
N_MEM = 256
GRID_W = 64
EPS = 1e-6
NEG_INF = -1e30
LOG2_E = 1.4426950408889634

MLA_HEADS = 8
MLA_Q_LORA = 256
MLA_KV_LORA = 128
MLA_NOPE = 64
MLA_ROPE = 32
MLA_QK = MLA_NOPE + MLA_ROPE
MLA_V = 64
ROPE_THETA = 10000.0
MLA_OUT = MLA_HEADS * MLA_V

SWA_HEADS = 4
SWA_KV_HEADS = 2
SWA_GROUP = SWA_HEADS // SWA_KV_HEADS
SWA_DH = 64
SWA_WINDOW = 128
SWA_OUT = SWA_HEADS * SWA_DH

NA_HEADS = 4
NA_DH = 64
NA_KR = 8
NA_KC = 16
NA_OUT = NA_HEADS * NA_DH

MEM_HEADS = 4
MEM_DH = 64

LANES = 128
V7X_VMEM_BYTES = 64 * 2**20
VMEM_LIMIT = 56 * 2**20

FFN_TM = 512
PREP_TM = 512
MLA_TQ = 512
MLA_TK = 512
MLA_NQ = 4
MLA_HG = 4
MLA_ACC_ROWS = MLA_V + 16
SWA_TQ = 256
SWA_WIN = SWA_TQ + 2 * SWA_WINDOW
NA_TQ = 4 * GRID_W
NA_WROWS = 12
NA_WIN = NA_WROWS * GRID_W
TAIL_TM = 512


def _cparams(sem):
    return pltpu.CompilerParams(dimension_semantics=sem, vmem_limit_bytes=VMEM_LIMIT)


def _rms(x, g):
    ms = jnp.mean(x * x, axis=-1, keepdims=True)
    return x * lax.rsqrt(ms + EPS) * g


def _dot(a, b):
    return jnp.dot(a, b, preferred_element_type=F32)


def _dot_nt(a, b):
    return lax.dot_general(a, b, (((1,), (1,)), ((), ())), preferred_element_type=F32)


def _resident(shape):
    nd = len(shape)
    return pl.BlockSpec(shape, lambda *_: (0,) * nd)


def _ffn_kernel(x_ref, g_ref, wg_ref, wu_ref, wo_ref, o_ref):
    x = x_ref[...]
    xn = _rms(x, g_ref[...]).astype(BF16)
    g = _dot(xn, wg_ref[...])
    u = _dot(xn, wu_ref[...])
    a = (g * jax.nn.sigmoid(g) * u).astype(BF16)
    o_ref[...] = x + 0.5 * _dot(a, wo_ref[...])


def _ffn(x2, g, wg, wu, wo):
    n = x2.shape[0]
    single = pl.Buffered(1)
    return pl.pallas_call(
        _ffn_kernel,
        grid=(n // FFN_TM,),
        in_specs=[
            pl.BlockSpec((FFN_TM, D_MODEL), lambda i: (i, 0)),
            _resident((1, D_MODEL)),
            pl.BlockSpec((D_MODEL, D_FF), lambda i: (0, 0), pipeline_mode=single),
            pl.BlockSpec((D_MODEL, D_FF), lambda i: (0, 0), pipeline_mode=single),
            pl.BlockSpec((D_FF, D_MODEL), lambda i: (0, 0), pipeline_mode=single),
        ],
        out_specs=pl.BlockSpec((FFN_TM, D_MODEL), lambda i: (i, 0)),
        out_shape=jax.ShapeDtypeStruct((n, D_MODEL), F32),
        compiler_params=_cparams(("parallel",)),
        name="ffn",
    )(x2, g, wg, wu, wo)


def _prep_kernel(x_ref, mixg_ref, wmla_ref, qng_ref, kvng_ref, wuq_ref, wuk_ref, wuvt_ref,
                 qgain_ref, kgain_ref, rc_ref, rs_ref, rct_ref, rst_ref,
                 wswa_ref, sqg_ref, skg_ref, wna_ref, nqg_ref, nkg_ref, bd_ref,
                 q_ref, k_ref, vt_ref, sq_ref, sk_ref, sv_ref, nq_ref, nk_ref, nv_ref):
    xn = _rms(x_ref[0], mixg_ref[...]).astype(BF16)

    z = _dot(xn, wmla_ref[...])
    cqn = _rms(z[:, :MLA_Q_LORA], qng_ref[...]).astype(BF16)
    ckvn = _rms(z[:, MLA_Q_LORA:MLA_Q_LORA + MLA_KV_LORA], kvng_ref[...]).astype(BF16)
    kr = z[:, MLA_Q_LORA + MLA_KV_LORA:]
    qat = _dot_nt(wuq_ref[...], cqn)
    ka = _dot(ckvn, wuk_ref[...])
    vt_ref[0, 0] = _dot_nt(wuvt_ref[...], ckvn).astype(BF16)

    rc, rs, kgain = rc_ref[...], rs_ref[...], kgain_ref[...]
    for h in range(MLA_HEADS):
        t = ka[:, h * LANES:(h + 1) * LANES] + kr
        ss = jnp.sum(t * t, axis=-1, keepdims=True)
        tn = t * lax.rsqrt(ss * (1.0 / MLA_QK) + EPS) * kgain
        k_ref[0, h] = (tn * rc + pltpu.roll(tn, LANES // 2, 1) * rs).astype(BF16)

    rct, rst, qgain = rct_ref[...], rst_ref[...], qgain_ref[...]
    for h in range(MLA_HEADS):
        t = qat[h * LANES:(h + 1) * LANES, :]
        ss = jnp.sum(t * t, axis=0, keepdims=True)
        tn = t * lax.rsqrt(ss * (1.0 / MLA_QK) + EPS) * qgain
        swapped = jnp.concatenate([tn[LANES // 2:], tn[:LANES // 2]], axis=0)
        q_ref[0, 0, h] = ((tn * rct + swapped * rst) * (MLA_QK ** -0.5 * LOG2_E)).astype(BF16)

    bd = bd_ref[...]

    def head_norm(t, gain):
        ss = _dot((t * t).astype(BF16), bd)
        return t * lax.rsqrt(ss * (1.0 / SWA_DH) + EPS) * gain

    zs = _dot(xn, wswa_ref[...])
    sq_ref[0] = (head_norm(zs[:, :256], sqg_ref[...]) * (SWA_DH ** -0.5)).astype(BF16)
    sk_ref[0] = head_norm(zs[:, 256:512], skg_ref[...]).astype(BF16)
    sv_ref[0] = zs[:, 512:].astype(BF16)

    zn = _dot(xn, wna_ref[...])
    nq_ref[0] = (head_norm(zn[:, :256], nqg_ref[...]) * (NA_DH ** -0.5)).astype(BF16)
    nk_ref[0] = head_norm(zn[:, 256:512], nkg_ref[...]).astype(BF16)
    nv_ref[0] = zn[:, 512:].astype(BF16)


def _prep(x3, p):
    b, s, _ = x3.shape
    nt = s // PREP_TM
    tok = lambda w: pl.BlockSpec((1, PREP_TM, w), lambda bi, i: (bi, i, 0))
    rope = pl.BlockSpec((PREP_TM, LANES), lambda bi, i: (i, 0))
    rope_t = pl.BlockSpec((LANES, PREP_TM), lambda bi, i: (0, i))
    weights = [p["mix_norm"], p["w_mla"], p["mla_q_norm"], p["mla_kv_norm"], p["w_uq_t"], p["w_uk"], p["w_uvt"],
               p["mla_q_gain_col"], p["mla_k_gain"]]
    tail = [p["w_swa"], p["swa_q_gain"], p["swa_k_gain"], p["w_na"], p["na_q_gain"], p["na_k_gain"], p["bd"]]
    in_specs = ([tok(D_MODEL)] + [_resident(w.shape) for w in weights] + [rope, rope, rope_t, rope_t]
                + [_resident(w.shape) for w in tail])
    out_specs = [pl.BlockSpec((1, 1, MLA_HEADS, LANES, PREP_TM), lambda bi, i: (bi, i, 0, 0, 0)),
                 pl.BlockSpec((1, MLA_HEADS, PREP_TM, LANES), lambda bi, i: (bi, 0, i, 0)),
                 pl.BlockSpec((1, 1, MLA_OUT, PREP_TM), lambda bi, i: (bi, i, 0, 0))] + [tok(256)] * 6
    tok_shape = jax.ShapeDtypeStruct((b, s, 256), BF16)
    out_shape = [jax.ShapeDtypeStruct((b, nt, MLA_HEADS, LANES, PREP_TM), BF16),
                 jax.ShapeDtypeStruct((b, MLA_HEADS, s, LANES), BF16),
                 jax.ShapeDtypeStruct((b, nt, MLA_OUT, PREP_TM), BF16)] + [tok_shape] * 6
    return pl.pallas_call(
        _prep_kernel,
        grid=(b, nt),
        in_specs=in_specs,
        out_specs=out_specs,
        out_shape=out_shape,
        compiler_params=_cparams(("parallel", "parallel")),
        name="mix_prep",
    )(x3, *weights, p["rope_c"], p["rope_s"], p["rope_ct"], p["rope_st"], *tail)


def _mla_kernel(q_ref, k_ref, vt_ref, o_ref, s_scr, p_scr, bmax_scr, alpha_scr, acc_scr, m_scr, *, nkb):
    heads = range(MLA_HG)
    n_items = MLA_NQ * nkb
    ones_rows = jnp.ones((MLA_ACC_ROWS - MLA_V, MLA_TK), BF16)

    def split(n):
        if isinstance(n, int):
            return n // nkb, n % nkb
        return lax.div(n, nkb), lax.rem(n, nkb)

    def scores(n, slot, hs=heads):
        qb, kb = split(n)
        row0 = kb * MLA_TK if isinstance(kb, int) else pl.multiple_of(kb * MLA_TK, MLA_TK)
        for hh in hs:
            s = _dot(k_ref[0, hh, pl.ds(row0, MLA_TK), :], q_ref[0, qb, hh])
            s_scr[slot, hh] = s
            bmax_scr[slot, hh] = jnp.max(s, axis=0, keepdims=True)

    def softmax(n, slot, hs=heads):
        first_key_block = split(n)[1] == 0
        for hh in hs:
            m_old = jnp.where(first_key_block, NEG_INF, m_scr[hh])
            m_new = jnp.maximum(m_old, bmax_scr[slot, hh])
            m_scr[hh] = m_new
            alpha_scr[slot, hh] = jnp.exp2(m_old - m_new)
            p_scr[slot, hh] = jnp.exp2(s_scr[slot, hh] - m_new).astype(BF16)

    def accumulate(n, slot, hs=heads):
        kb = split(n)[1]
        for hh in hs:
            vt1 = jnp.concatenate([vt_ref[0, kb, hh * MLA_V:(hh + 1) * MLA_V, :], ones_rows], axis=0)
            acc_scr[hh] = alpha_scr[slot, hh] * acc_scr[hh] + _dot(vt1, p_scr[slot, hh])

    def flush(n):
        qb = split(n)[0]
        q0 = qb * MLA_TQ if isinstance(qb, int) else pl.multiple_of(qb * MLA_TQ, MLA_TQ)
        for hp in range(MLA_HG // 2):
            pair_t = jnp.concatenate(
                [acc_scr[hh, :MLA_V] / acc_scr[hh, MLA_V:MLA_V + 1] for hh in (2 * hp, 2 * hp + 1)], axis=0)
            o_ref[0, pl.ds(q0, MLA_TQ), hp * LANES:(hp + 1) * LANES] = pair_t.T

    def step(n, slot):
        for hh in heads:
            scores(n, slot, (hh,))
            softmax(n - 1, 1 - slot, (hh,))
            accumulate(n - 1, 1 - slot, (hh,))
        pl.when(split(n)[1] == 0)(functools.partial(flush, n - 1))

    acc_scr[...] = jnp.zeros_like(acc_scr)
    m_scr[...] = jnp.full_like(m_scr, NEG_INF)
    scores(0, 0)

    def trip(n, carry):
        for slot in range(2):
            pl.when(jnp.bitwise_and(n, 1) == slot)(functools.partial(step, n, slot))
        return carry

    lax.fori_loop(1, n_items, trip, 0)
    for hh in heads:
        softmax(n_items - 1, 1, (hh,))
        accumulate(n_items - 1, 1, (hh,))
    flush(n_items - 1)


def _mla_attn(q, k, vt):
    b, h, s, _ = k.shape
    nkb = s // MLA_TK
    tq_step = MLA_NQ * MLA_TQ
    assert (MLA_NQ * nkb) % 2 == 0 and h % MLA_HG == 0 and s % tq_step == 0
    assert MLA_TK == PREP_TM and MLA_TQ == PREP_TM
    return pl.pallas_call(
        functools.partial(_mla_kernel, nkb=nkb),
        grid=(b, h // MLA_HG, s // tq_step),
        in_specs=[
            pl.BlockSpec((1, MLA_NQ, MLA_HG, LANES, MLA_TQ), lambda bi, hg, i: (bi, i, hg, 0, 0)),
            pl.BlockSpec((1, MLA_HG, s, LANES), lambda bi, hg, i: (bi, hg, 0, 0)),
            pl.BlockSpec((1, nkb, MLA_HG * MLA_V, MLA_TK), lambda bi, hg, i: (bi, 0, hg, 0)),
        ],
        out_specs=pl.BlockSpec((1, tq_step, MLA_HG * MLA_V), lambda bi, hg, i: (bi, i, hg)),
        out_shape=jax.ShapeDtypeStruct((b, s, MLA_OUT), F32),
        scratch_shapes=[pltpu.VMEM((2, MLA_HG, MLA_TK, MLA_TQ), F32),
                        pltpu.VMEM((2, MLA_HG, MLA_TK, MLA_TQ), BF16),
                        pltpu.VMEM((2, MLA_HG, 1, MLA_TQ), F32),
                        pltpu.VMEM((2, MLA_HG, 1, MLA_TQ), F32),
                        pltpu.VMEM((MLA_HG, MLA_ACC_ROWS, MLA_TQ), F32),
                        pltpu.VMEM((MLA_HG, 1, MLA_TQ), F32)],
        compiler_params=_cparams(("parallel", "parallel", "arbitrary")),
        name="mla_attn",
    )(q, k, vt)


def _lane_head_mask(h):
    lane = lax.broadcasted_iota(jnp.int32, (1, 256), 1)
    return (lane >= h * 64) & (lane < (h + 1) * 64)


def _window_attn(q, kw, vw, bias_ref, sinks):
    o = jnp.zeros((q.shape[0], 256), F32)
    score = lambda h: _dot_nt(jnp.where(_lane_head_mask(h), q, jnp.zeros_like(q)), kw) + bias_ref[0, h]
    s_next = score(0)
    for h in range(4):
        mask = _lane_head_mask(h)
        s = s_next
        if h + 1 < 4:
            s_next = score(h + 1)
        m = jnp.max(s, axis=-1, keepdims=True)
        if sinks is not None:
            m = jnp.maximum(m, sinks[h])
        p = jnp.exp(s - m)
        l = jnp.sum(p, axis=-1, keepdims=True)
        if sinks is not None:
            l = l + jnp.exp(sinks[h] - m)
        o = o + jnp.where(mask, _dot(p.astype(BF16), vw) * (1.0 / l), 0.0)
    return o


def _swa_kernel(sink_ref, q_ref, k_ref, v_ref, bias_ref, g_ref, o_ref, *, seq):
    i = pl.program_id(1)
    w0 = pl.multiple_of(jnp.clip(i * SWA_TQ - SWA_WINDOW, 0, seq - SWA_WIN), SWA_WINDOW)
    kw = k_ref[0, pl.ds(w0, SWA_WIN), :]
    vw = v_ref[0, pl.ds(w0, SWA_WIN), :]
    sinks = [sink_ref[h] for h in range(SWA_HEADS)]
    o = _window_attn(q_ref[0], kw, vw, bias_ref, sinks)
    o_ref[0] = _rms(o, g_ref[...]).astype(BF16)


def _na_kernel(q_ref, k_ref, v_ref, bias_ref, g_ref, o_ref, *, rows):
    i = pl.program_id(1)
    r0 = jnp.clip(i * (NA_TQ // GRID_W) - NA_KR // 2, 0, rows - NA_WROWS)
    w0 = pl.multiple_of(r0 * GRID_W, GRID_W)
    kw = k_ref[0, pl.ds(w0, NA_WIN), :]
    vw = v_ref[0, pl.ds(w0, NA_WIN), :]
    o = _window_attn(q_ref[0], kw, vw, bias_ref, None)
    o_ref[0] = _rms(o, g_ref[...]).astype(BF16)


def _edge_pattern(i, n):
    return (i > 0).astype(jnp.int32) + (i == n - 1).astype(jnp.int32)


def _swa_attn(q, k, v, bias, sink, g):
    b, s, _ = q.shape
    nq = s // SWA_TQ
    return pl.pallas_call(
        functools.partial(_swa_kernel, seq=s),
        grid=(b, nq),
        in_specs=[
            pl.BlockSpec(memory_space=pltpu.SMEM),
            pl.BlockSpec((1, SWA_TQ, 256), lambda bi, i: (bi, i, 0)),
            pl.BlockSpec((1, s, 256), lambda bi, i: (bi, 0, 0)),
            pl.BlockSpec((1, s, 256), lambda bi, i: (bi, 0, 0)),
            pl.BlockSpec((1, SWA_HEADS, SWA_TQ, SWA_WIN), lambda bi, i: (_edge_pattern(i, nq), 0, 0, 0)),
            _resident((1, SWA_OUT)),
        ],
        out_specs=pl.BlockSpec((1, SWA_TQ, 256), lambda bi, i: (bi, i, 0)),
        out_shape=jax.ShapeDtypeStruct((b, s, SWA_OUT), BF16),
        compiler_params=_cparams(("parallel", "arbitrary")),
        name="swa_attn",
    )(sink, q, k, v, bias, g)


def _na_attn(q, k, v, bias, g):
    b, s, _ = q.shape
    nq = s // NA_TQ
    return pl.pallas_call(
        functools.partial(_na_kernel, rows=s // GRID_W),
        grid=(b, nq),
        in_specs=[
            pl.BlockSpec((1, NA_TQ, 256), lambda bi, i: (bi, i, 0)),
            pl.BlockSpec((1, s, 256), lambda bi, i: (bi, 0, 0)),
            pl.BlockSpec((1, s, 256), lambda bi, i: (bi, 0, 0)),
            pl.BlockSpec((1, NA_HEADS, NA_TQ, NA_WIN), lambda bi, i: (_edge_pattern(i, nq), 0, 0, 0)),
            _resident((1, NA_OUT)),
        ],
        out_specs=pl.BlockSpec((1, NA_TQ, 256), lambda bi, i: (bi, i, 0)),
        out_shape=jax.ShapeDtypeStruct((b, s, NA_OUT), BF16),
        compiler_params=_cparams(("parallel", "arbitrary")),
        name="na_attn",
    )(q, k, v, bias, g)


def _swa_bias_table(seq):
    slopes = 2.0 ** (-8.0 * np.arange(1, SWA_HEADS + 1, dtype=np.float64) / SWA_HEADS)
    ql = np.arange(SWA_TQ)[:, None]
    kl = np.arange(SWA_WIN)[None, :]
    pats = []
    for q0, w0 in ((0, 0), (SWA_TQ, SWA_TQ - SWA_WINDOW), (seq - SWA_TQ, seq - SWA_WIN)):
        dist = np.abs(ql + q0 - (kl + w0))
        pats.append(np.where(dist[None] <= SWA_WINDOW, -slopes[:, None, None] * dist[None], NEG_INF))
    return jnp.asarray(np.stack(pats), F32)


def _na_bias_table(rel_bias, rows):
    qr_l = np.arange(NA_TQ // GRID_W)
    kr_l = np.arange(NA_WROWS)
    qc = np.arange(GRID_W)[:, None]
    kc = np.arange(GRID_W)[None, :]
    c0 = np.clip(qc - NA_KC // 2, 0, GRID_W - NA_KC)
    col_ok = (kc >= c0) & (kc < c0 + NA_KC)
    dc = np.clip(kc - qc + (NA_KC - 1), 0, 2 * NA_KC - 2)
    pick_dc = (dc[None] == np.arange(2 * NA_KC - 1)[:, None, None]).astype(np.float32)
    last_first = rows - NA_TQ // GRID_W
    pick_dr, ok = [], []
    for r_first, w_row0 in ((0, 0), (NA_KR // 2, 0), (last_first, rows - NA_WROWS)):
        rq = r_first + qr_l[:, None]
        rk = w_row0 + kr_l[None, :]
        r0 = np.clip(rq - NA_KR // 2, 0, rows - NA_KR)
        row_ok = (rk >= r0) & (rk < r0 + NA_KR)
        dr = np.clip(rk - rq + (NA_KR - 1), 0, 2 * NA_KR - 2)
        pick_dr.append((dr[None] == np.arange(2 * NA_KR - 1)[:, None, None]).astype(np.float32))
        ok.append(row_ok[:, None, :, None] & col_ok[None, :, None, :])
    hi = lax.Precision.HIGHEST
    t = jnp.einsum("hdc,pdqk->phqkc", rel_bias, jnp.asarray(np.stack(pick_dr)), precision=hi)
    t = jnp.einsum("phqkc,cxy->phqxky", t, jnp.asarray(pick_dc), precision=hi)
    t = jnp.where(jnp.asarray(np.stack(ok))[:, None], t, NEG_INF)
    return t.reshape(3, NA_HEADS, NA_TQ, NA_WIN).astype(F32)


def _memkv_kernel(mem_ref, g_ref, w_ref, kg_ref, bd_ref, k_ref, v_ref):
    m = _rms(mem_ref[0], g_ref[...]).astype(BF16)
    kv = _dot(m, w_ref[...])
    k = kv[:, :256]
    ss = _dot((k * k).astype(BF16), bd_ref[...])
    k = (k * lax.rsqrt(ss * (1.0 / MEM_DH) + EPS) * kg_ref[...]).astype(BF16)
    v = kv[:, 256:].astype(BF16)
    for h in range(MEM_HEADS):
        mask = _lane_head_mask(h)
        k_ref[0, h] = jnp.where(mask, k, jnp.zeros_like(k))
        v_ref[0, h] = jnp.where(mask, v, jnp.zeros_like(v))


def _mem_kv(mem, g, w, kg, bd):
    b, m, _ = mem.shape
    out = jax.ShapeDtypeStruct((b, MEM_HEADS, m, 256), BF16)
    spec = pl.BlockSpec((1, MEM_HEADS, m, 256), lambda bi: (bi, 0, 0, 0))
    return pl.pallas_call(
        _memkv_kernel,
        grid=(b,),
        in_specs=[pl.BlockSpec((1, m, D_MODEL), lambda bi: (bi, 0, 0)), _resident(g.shape), _resident(w.shape),
                  _resident(kg.shape), _resident(bd.shape)],
        out_specs=[spec, spec],
        out_shape=[out, out],
        compiler_params=_cparams(("parallel",)),
        name="mem_kv",
    )(mem, g, w, kg, bd)


def _tail_kernel(x_ref, om_ref, os_ref, on_ref, gm_ref, wmix_ref,
                 mg_ref, wq_ref, qg_ref, bd_ref, k_ref, v_ref, wmo_ref,
                 fg_ref, wg_ref, wu_ref, wo_ref, bg_ref, o_ref):
    om = _rms(om_ref[0], gm_ref[...]).astype(BF16)
    cat = jnp.concatenate([om, os_ref[0], on_ref[0]], axis=-1)
    x = x_ref[0] + _dot(cat, wmix_ref[...])

    xn = _rms(x, mg_ref[...]).astype(BF16)
    q = _dot(xn, wq_ref[...])
    ss = _dot((q * q).astype(BF16), bd_ref[...])
    q = (q * lax.rsqrt(ss * (1.0 / MEM_DH) + EPS) * qg_ref[...] * (MEM_DH ** -0.5)).astype(BF16)
    o = jnp.zeros((x.shape[0], 256), F32)
    s_next = _dot_nt(q, k_ref[0, 0])
    for h in range(MEM_HEADS):
        s = s_next
        if h + 1 < MEM_HEADS:
            s_next = _dot_nt(q, k_ref[0, h + 1])
        m = jnp.max(s, axis=-1, keepdims=True)
        p = jnp.exp(s - m)
        l = jnp.sum(p, axis=-1, keepdims=True)
        o = o + _dot(p.astype(BF16), v_ref[0, h]) * (1.0 / l)
    x = x + _dot(o.astype(BF16), wmo_ref[...])

    xn = _rms(x, fg_ref[...]).astype(BF16)
    g = _dot(xn, wg_ref[...])
    u = _dot(xn, wu_ref[...])
    a = (g * jax.nn.sigmoid(g) * u).astype(BF16)
    x = x + 0.5 * _dot(a, wo_ref[...])
    o_ref[0] = _rms(x, bg_ref[...])


def _tail(x3, o_mla, o_swa, o_na, mk, mv, p):
    b, s, _ = x3.shape
    m = mk.shape[2]
    tok = lambda w: pl.BlockSpec((1, TAIL_TM, w), lambda bi, i: (bi, i, 0))
    kv = pl.BlockSpec((1, MEM_HEADS, m, 256), lambda bi, i: (bi, 0, 0, 0))
    single = lambda a: pl.BlockSpec(a.shape, lambda bi, i: (0,) * a.ndim, pipeline_mode=pl.Buffered(1))
    fg, wg, wu, wo = p["ffn2"]
    mix = [p["g_mla"], p["w_mix_out"]]
    mem = [p["mem_norm_x"], p["mem_w_q"], p["mem_q_gain"], p["bd"]]
    ffn = [fg, wg, wu, wo, p["block_norm"]]
    in_specs = ([tok(D_MODEL), tok(MLA_OUT), tok(SWA_OUT), tok(NA_OUT)] + [single(a) for a in mix]
                + [single(a) for a in mem] + [kv, kv, single(p["mem_w_o"])] + [single(a) for a in ffn])
    return pl.pallas_call(
        _tail_kernel,
        grid=(b, s // TAIL_TM),
        in_specs=in_specs,
        out_specs=tok(D_MODEL),
        out_shape=jax.ShapeDtypeStruct((b, s, D_MODEL), F32),
        compiler_params=_cparams(("parallel", "parallel")),
        name="layer_tail",
    )(x3, o_mla, o_swa, o_na, *mix, *mem, mk, mv, p["mem_w_o"], *ffn)


def _head_lanes(a):
    half = MLA_ROPE // 2
    x1 = a[..., MLA_NOPE:MLA_NOPE + half]
    x2 = a[..., MLA_NOPE + half:MLA_QK]
    split = LANES // 2 - half
    pad = jnp.zeros(a.shape[:-1] + (LANES - MLA_QK,), a.dtype)
    return jnp.concatenate([x2, a[..., :split], x1, a[..., split:MLA_NOPE], pad], axis=-1)


def _rope_tables(seq):
    inv = 1.0 / (ROPE_THETA ** (jnp.arange(0, MLA_ROPE, 2, dtype=F32) / MLA_ROPE))
    ang = jnp.arange(seq, dtype=F32)[:, None] * inv[None, :]
    cos, sin = jnp.cos(ang), jnp.sin(ang)
    rc = _head_lanes(jnp.concatenate([jnp.ones((seq, MLA_NOPE), F32), cos, cos], axis=1))
    rs = _head_lanes(jnp.concatenate([jnp.zeros((seq, MLA_NOPE), F32), -sin, sin], axis=1))
    return rc, rs


def _layer_params(l, seq, a):
    row = lambda v: v[l].reshape(1, -1).astype(F32)
    p = {}
    for name in ("ffn1", "ffn2"):
        w_in = a[name + "_w_in"][l]
        p[name] = (row(a[name + "_norm"]), w_in[:, :D_FF].astype(BF16), w_in[:, D_FF:].astype(BF16),
                   a[name + "_w_out"][l].astype(BF16))
    w = a["w_mix_in"][l]
    o_ckv = MLA_Q_LORA
    o_kr = o_ckv + MLA_KV_LORA
    o_swa = o_kr + MLA_ROPE
    o_na = o_swa + (SWA_HEADS + 2 * SWA_KV_HEADS) * SWA_DH
    w_kr = _head_lanes(jnp.pad(w[:, o_kr:o_swa], ((0, 0), (MLA_NOPE, 0))))
    p["w_mla"] = jnp.concatenate([w[:, :o_kr], w_kr], axis=1).astype(BF16)
    p["mix_norm"] = row(a["mix_norm"])
    p["mla_q_norm"] = row(a["mla_q_norm"])
    p["mla_kv_norm"] = row(a["mla_kv_norm"])
    w_uq = a["mla_w_uq"][l].reshape(MLA_Q_LORA, MLA_HEADS, MLA_QK)
    p["w_uq_t"] = _head_lanes(w_uq).reshape(MLA_Q_LORA, MLA_HEADS * LANES).T.astype(BF16)
    w_ukv = a["mla_w_ukv"][l].reshape(MLA_KV_LORA, MLA_HEADS, MLA_NOPE + MLA_V)
    w_uk = jnp.pad(w_ukv[:, :, :MLA_NOPE], ((0, 0), (0, 0), (0, MLA_ROPE)))
    p["w_uk"] = _head_lanes(w_uk).reshape(MLA_KV_LORA, MLA_HEADS * LANES).astype(BF16)
    p["w_uvt"] = w_ukv[:, :, MLA_NOPE:].reshape(MLA_KV_LORA, MLA_OUT).T.astype(BF16)
    p["mla_q_gain_col"] = _head_lanes(row(a["mla_q_gain"])).reshape(LANES, 1)
    p["mla_k_gain"] = _head_lanes(row(a["mla_k_gain"]))
    p["rope_c"], p["rope_s"] = _rope_tables(seq)
    p["rope_ct"], p["rope_st"] = p["rope_c"].T, p["rope_s"].T
    w_sq = w[:, o_swa:o_swa + SWA_OUT]
    w_sk = w[:, o_swa + SWA_OUT:o_swa + SWA_OUT + SWA_KV_HEADS * SWA_DH].reshape(D_MODEL, SWA_KV_HEADS, SWA_DH)
    w_sv = w[:, o_swa + SWA_OUT + SWA_KV_HEADS * SWA_DH:o_na].reshape(D_MODEL, SWA_KV_HEADS, SWA_DH)
    expand = lambda t: jnp.repeat(t, SWA_GROUP, axis=1).reshape(D_MODEL, SWA_OUT)
    p["w_swa"] = jnp.concatenate([w_sq, expand(w_sk), expand(w_sv)], axis=1).astype(BF16)
    p["swa_q_gain"] = jnp.tile(row(a["swa_q_gain"]), (1, SWA_HEADS))
    p["swa_k_gain"] = jnp.tile(row(a["swa_k_gain"]), (1, SWA_HEADS))
    p["swa_sink"] = a["swa_sink"][l].astype(F32)
    p["w_na"] = w[:, o_na:].astype(BF16)
    p["na_q_gain"] = jnp.tile(row(a["na_q_gain"]), (1, NA_HEADS))
    p["na_k_gain"] = jnp.tile(row(a["na_k_gain"]), (1, NA_HEADS))
    p["na_bias"] = _na_bias_table(a["na_rel_bias"][l].astype(F32), seq // GRID_W)
    g = row(a["grp_out_gain"])
    p["g_mla"], p["g_swa"], p["g_na"] = g[:, :MLA_OUT], g[:, MLA_OUT:MLA_OUT + SWA_OUT], g[:, MLA_OUT + SWA_OUT:]
    p["w_mix_out"] = a["w_mix_out"][l].astype(BF16)
    p["mem_norm_x"] = row(a["mem_norm_x"])
    p["mem_norm_m"] = row(a["mem_norm_m"])
    p["mem_w_q"] = a["mem_w_q"][l].astype(BF16)
    p["mem_w_kv"] = a["mem_w_kv"][l].astype(BF16)
    p["mem_q_gain"] = jnp.tile(row(a["mem_q_gain"]), (1, MEM_HEADS))
    p["mem_k_gain"] = jnp.tile(row(a["mem_k_gain"]), (1, MEM_HEADS))
    p["mem_w_o"] = a["mem_w_o"][l].astype(BF16)
    p["block_norm"] = row(a["block_norm"])
    seg = np.arange(256) // 64
    p["bd"] = jnp.asarray(seg[:, None] == seg[None, :], BF16)
    return p


def kernel(x, mem, ffn1_norm, ffn1_w_in, ffn1_w_out, mix_norm, w_mix_in, mla_q_norm, mla_w_uq, mla_kv_norm, mla_w_ukv, mla_q_gain, mla_k_gain, swa_q_gain, swa_k_gain, swa_sink, na_q_gain, na_k_gain, na_rel_bias, grp_out_gain, w_mix_out, mem_norm_x, mem_norm_m, mem_w_q, mem_w_kv, mem_q_gain, mem_k_gain, mem_w_o, ffn2_norm, ffn2_w_in, ffn2_w_out, block_norm):
    a = dict(ffn1_norm=ffn1_norm, ffn1_w_in=ffn1_w_in, ffn1_w_out=ffn1_w_out, mix_norm=mix_norm, w_mix_in=w_mix_in,
             mla_q_norm=mla_q_norm, mla_w_uq=mla_w_uq, mla_kv_norm=mla_kv_norm, mla_w_ukv=mla_w_ukv,
             mla_q_gain=mla_q_gain, mla_k_gain=mla_k_gain, swa_q_gain=swa_q_gain, swa_k_gain=swa_k_gain,
             swa_sink=swa_sink, na_q_gain=na_q_gain, na_k_gain=na_k_gain, na_rel_bias=na_rel_bias,
             grp_out_gain=grp_out_gain, w_mix_out=w_mix_out, mem_norm_x=mem_norm_x, mem_norm_m=mem_norm_m,
             mem_w_q=mem_w_q, mem_w_kv=mem_w_kv, mem_q_gain=mem_q_gain, mem_k_gain=mem_k_gain, mem_w_o=mem_w_o,
             ffn2_norm=ffn2_norm, ffn2_w_in=ffn2_w_in, ffn2_w_out=ffn2_w_out, block_norm=block_norm)
    b, s, d = x.shape
    depth = ffn1_norm.shape[0]
    swa_bias = _swa_bias_table(s)
    x = x.astype(F32)
    mem = mem.astype(F32)
    for l in range(depth):
        p = _layer_params(l, s, a)
        x2 = _ffn(x.reshape(b * s, d), *p["ffn1"])
        q, k, vt, sq, sk, sv, nq, nk, nv = _prep(x2.reshape(b, s, d), p)
        o_mla = _mla_attn(q, k, vt)
        o_swa = _swa_attn(sq, sk, sv, swa_bias, p["swa_sink"], p["g_swa"])
        o_na = _na_attn(nq, nk, nv, p["na_bias"], p["g_na"])
        mk, mv = _mem_kv(mem, p["mem_norm_m"], p["mem_w_kv"], p["mem_k_gain"], p["bd"])
        x = _tail(x2.reshape(b, s, d), o_mla, o_swa, o_na, mk, mv, p)
    return x
```
